```python
import jax, jax.numpy as jnp
from jax import lax
import numpy as np

D_MODEL = 2048
BATCH = 8
SEQ = 2048
DEPTH = 2
DEC_BATCH = 128
DEC_SEQ = 1
PAST_LEN = 2048
PAGE_SIZE = 128

HEAD_DIM = 128
H_FOX = 8
H_SB = 8
H_ATT = H_FOX + H_SB
D_ATT = H_ATT * HEAD_DIM
ATT_SCALE = HEAD_DIM ** -0.5
Q_BLOCK = 128
NEG_INF = -1e30
N_ATT_LAYERS = (DEPTH + 1) // 2
N_RWKV_LAYERS = DEPTH // 2
RWKV_HS = 64
H_RWKV = D_MODEL // RWKV_HS
R_DECAY = 96
R_AAA = 96
R_GATE = 256
LNX_EPS = 64e-5
PEER_HEADS = 8
PEER_TOPK = 16
N_KEYS = 128
N_EXPERTS = N_KEYS * N_KEYS
D_KEY = 256
PEER_CHUNK = 128
N_MOD = 6
RMS_EPS = 1e-6

kernel_name = "fox_stickbreak_rwkv7_peer_adaln_step"


def rmsnorm(x, g):
    xf = x.astype(jnp.float32)
    return (xf * lax.rsqrt(jnp.mean(xf * xf, -1, keepdims=True) + RMS_EPS) * g).astype(x.dtype)


def ada_mod(c, w, b):
    return (jax.nn.silu(c) @ w + b).reshape(c.shape[0], N_MOD, D_MODEL)


def modulate(h, shift, scale):
    return h * (1 + scale[:, None, :]) + shift[:, None, :]


def att_project(h, w_in, q_gain, k_gain, f_bias):
    B, T, _ = h.shape
    proj = h @ w_in
    q = proj[..., :D_ATT].reshape(B, T, H_ATT, HEAD_DIM)
    k = proj[..., D_ATT:2 * D_ATT].reshape(B, T, H_ATT, HEAD_DIM)
    v = proj[..., 2 * D_ATT:3 * D_ATT].reshape(B, T, H_ATT, HEAD_DIM)
    logf = jax.nn.log_sigmoid(proj[..., 3 * D_ATT:].astype(jnp.float32) + f_bias)
    q = jnp.concatenate([rmsnorm(q[:, :, :H_FOX], q_gain), q[:, :, H_FOX:]], axis=2)
    k = jnp.concatenate([rmsnorm(k[:, :, :H_FOX], k_gain), k[:, :, H_FOX:]], axis=2)
    return q, k, v, logf


def att_weights(s, f_q, f_k, q_pos, k_pos):
    decay = jnp.swapaxes(f_q, 1, 2)[..., :, None] - jnp.swapaxes(f_k, 1, 2)[..., None, :]
    causal = k_pos[None, :] <= q_pos[:, None]
    p_fox = jax.nn.softmax(jnp.where(causal, s[:, :H_FOX] + decay, NEG_INF), axis=-1)
    z = s[:, H_FOX:]
    strict = k_pos[None, :] < q_pos[:, None]
    log_keep = jnp.where(strict, jax.nn.log_sigmoid(-z), 0.0)
    later = lax.cumsum(log_keep, axis=3, reverse=True) - log_keep
    a_sb = jnp.where(strict, jnp.exp(jax.nn.log_sigmoid(z) + later), 0.0)
    return jnp.concatenate([p_fox, a_sb], axis=1)


def attn_prompt(h, w_in, q_gain, k_gain, f_bias, w_o):
    B, S, _ = h.shape
    q, k, v, logf = att_project(h, w_in, q_gain, k_gain, f_bias)
    f_cum = lax.cumsum(logf, axis=1)
    nb = S // Q_BLOCK
    q_blocks = jnp.moveaxis(q.reshape(B, nb, Q_BLOCK, H_ATT, HEAD_DIM), 1, 0)
    f_blocks = jnp.moveaxis(f_cum.reshape(B, nb, Q_BLOCK, H_FOX), 1, 0)
    starts = jnp.arange(nb, dtype=jnp.int32) * Q_BLOCK
    k_pos = jnp.arange(S, dtype=jnp.int32)

    def block(args):
        qi, fi, start = args
        s = jnp.einsum('bqhd,bkhd->bhqk', qi, k, preferred_element_type=jnp.float32) * ATT_SCALE
        w = att_weights(s, fi, f_cum, start + jnp.arange(Q_BLOCK, dtype=jnp.int32), k_pos)
        return jnp.einsum('bhqk,bkhd->bqhd', w, v, preferred_element_type=jnp.float32)

    o = lax.map(block, (q_blocks, f_blocks, starts))
    o = jnp.moveaxis(o, 0, 1).reshape(B, S, D_ATT).astype(h.dtype)
    return o @ w_o, k, v, logf


def attn_sample(h, cache_k, cache_v, cache_logf, li, page_table, w_in, q_gain, k_gain, f_bias, w_o):
    B, T, _ = h.shape
    q, k, v, logf = att_project(h, w_in, q_gain, k_gain, f_bias)
    P = page_table.shape[1] * PAGE_SIZE
    k_past = cache_k[li, page_table].reshape(B, P, H_ATT, HEAD_DIM)
    v_past = cache_v[li, page_table].reshape(B, P, H_ATT, HEAD_DIM)
    lf_past = cache_logf[li, page_table].reshape(B, P, H_FOX).astype(jnp.float32)
    f_cum = lax.cumsum(jnp.concatenate([lf_past, logf], axis=1), axis=1)
    s = jnp.concatenate([
        jnp.einsum('bqhd,bkhd->bhqk', q, k_past, preferred_element_type=jnp.float32),
        jnp.einsum('bqhd,bkhd->bhqk', q, k, preferred_element_type=jnp.float32)], axis=-1) * ATT_SCALE
    q_pos = P + jnp.arange(T, dtype=jnp.int32)
    k_pos = jnp.arange(P + T, dtype=jnp.int32)
    w = att_weights(s, f_cum[:, P:], f_cum, q_pos, k_pos)
    o = (jnp.einsum('bhqk,bkhd->bqhd', w[..., :P], v_past, preferred_element_type=jnp.float32)
         + jnp.einsum('bhqk,bkhd->bqhd', w[..., P:], v, preferred_element_type=jnp.float32))
    return o.reshape(B, T, D_ATT).astype(h.dtype) @ w_o, k, v, logf


def rwkv7(h, shift0, state0, mix, w0, w1, w2, a0, a1, a2, g1, g2, k_k, k_a, r_k,
          w_r, w_k, w_v, w_o, ln_w, ln_b):
    B, T, _ = h.shape
    prev = jnp.concatenate([shift0[:, None, :].astype(h.dtype), h[:, :-1]], axis=1)
    xx = prev - h
    xr, xw, xk, xv, xa, xg = (h + xx * mix[j] for j in range(6))
    r = (xr @ w_r).astype(jnp.float32)
    k = (xk @ w_k).astype(jnp.float32)
    v = (xv @ w_v).astype(jnp.float32)
    wl = -jax.nn.softplus(-(w0 + jnp.tanh(xw @ w1) @ w2).astype(jnp.float32)) - 0.5
    decay = jnp.exp(-jnp.exp(wl))
    a = jax.nn.sigmoid((a0 + (xa @ a1) @ a2).astype(jnp.float32))
    g = jax.nn.sigmoid(xg @ g1) @ g2
    hd = lambda t: t.reshape(B, T, H_RWKV, RWKV_HS)
    kk = hd(k * k_k)
    kk = kk / jnp.maximum(jnp.sqrt(jnp.sum(kk * kk, -1, keepdims=True)), 1e-12)
    k = k * (1 + (a - 1) * k_a)
    r4, k4, v4, w4 = hd(r), hd(k), hd(v), hd(decay)
    aa, bb = -kk, kk * hd(a)

    def step(S, inp):
        rt, wt, kt, vt, at, bt = inp
        S = (S * wt[:, :, None, :]
             + jnp.einsum('bhvk,bhk->bhv', S, at)[..., None] * bt[:, :, None, :]
             + vt[..., None] * kt[:, :, None, :])
        return S, jnp.einsum('bhvk,bhk->bhv', S, rt)

    tm = lambda t: jnp.moveaxis(t, 1, 0)
    S_T, o = lax.scan(step, state0.astype(jnp.float32), (tm(r4), tm(w4), tm(k4), tm(v4), tm(aa), tm(bb)))
    o = jnp.moveaxis(o, 0, 1)
    mu = jnp.mean(o, -1, keepdims=True)
    var = jnp.mean(jnp.square(o - mu), -1, keepdims=True)
    o = ((o - mu) * lax.rsqrt(var + LNX_EPS)).reshape(B, T, D_MODEL) * ln_w + ln_b
    o = o + (jnp.sum(r4 * k4 * r_k, -1, keepdims=True) * v4).reshape(B, T, D_MODEL)
    return (o * g).astype(h.dtype) @ w_o, h[:, -1], S_T


def peer(h, w_q, sub_k1, sub_k2, u, v):
    B, T, _ = h.shape
    flat = h.reshape(B * T, D_MODEL)
    n = flat.shape[0]
    pad = (-n) % PEER_CHUNK
    chunks = jnp.pad(flat, ((0, pad), (0, 0))).reshape(-1, PEER_CHUNK, D_MODEL)

    def body(xc):
        q = (xc @ w_q).reshape(PEER_CHUNK, PEER_HEADS, 2, D_KEY // 2).astype(jnp.float32)
        s1 = jnp.einsum('chd,hnd->chn', q[:, :, 0], sub_k1)
        s2 = jnp.einsum('chd,hnd->chn', q[:, :, 1], sub_k2)
        v1, i1 = lax.top_k(s1, PEER_TOPK)
        v2, i2 = lax.top_k(s2, PEER_TOPK)
        cand_s = (v1[..., :, None] + v2[..., None, :]).reshape(PEER_CHUNK, PEER_HEADS, PEER_TOPK * PEER_TOPK)
        cand_e = (i1[..., :, None] * N_KEYS + i2[..., None, :]).reshape(PEER_CHUNK, PEER_HEADS, PEER_TOPK * PEER_TOPK)
        top_s, pos = lax.top_k(cand_s, PEER_TOPK)
        expert = jnp.take_along_axis(cand_e, pos, axis=-1)
        gate = jax.nn.softmax(top_s, axis=-1)
        u_sel = jnp.take(u, expert, axis=0)
        act = jax.nn.gelu(jnp.einsum('cd,chkd->chk', xc, u_sel, preferred_element_type=jnp.float32),
                          approximate=False)
        v_sel = jnp.take(v, expert, axis=0)
        return jnp.einsum('chk,chkd->cd', gate * act, v_sel, preferred_element_type=jnp.float32).astype(h.dtype)

    out = lax.map(body, chunks).reshape(-1, D_MODEL)[:n]
    return out.reshape(B, T, D_MODEL)


def setup_inputs(seed: int = 0) -> dict:
    key = jax.random.key(seed)
    keys = jax.random.split(key, 64)
    ctr = [0]

    def nk():
        k = keys[ctr[0]]
        ctr[0] += 1
        return k

    def nrm(shape, scale=1.0, mean=0.0):
        return mean + scale * jax.random.normal(nk(), shape, jnp.float32)

    D = D_MODEL
    n_pages = PAST_LEN // PAGE_SIZE
    n_used = DEC_BATCH * n_pages
    n_pool = n_used + max(1, n_used // 4)
    LA, LR = N_ATT_LAYERS, N_RWKV_LAYERS
    inp = {}
    inp["x_prompt"] = nrm((BATCH, SEQ, D))
    inp["x_sample"] = nrm((DEC_BATCH, DEC_SEQ, D))
    inp["cache_k"] = nrm((LA, n_pool, PAGE_SIZE, H_ATT, HEAD_DIM))
    inp["cache_v"] = nrm((LA, n_pool, PAGE_SIZE, H_ATT, HEAD_DIM))
    inp["cache_logf"] = jax.nn.log_sigmoid(nrm((LA, n_pool, PAGE_SIZE, H_FOX), 0.5, 3.0))
    inp["state_wkv"] = nrm((LR, DEC_BATCH, H_RWKV, RWKV_HS, RWKV_HS), 0.5)
    inp["state_shift"] = nrm((LR, DEC_BATCH, D))
    inp["page_table"] = jax.random.permutation(nk(), n_pool)[:n_used].reshape(DEC_BATCH, n_pages).astype(jnp.int32)
    inp["c_prompt"] = nrm((BATCH, D))
    inp["c_sample"] = nrm((DEC_BATCH, D))
    inp["ada_w"] = nrm((DEPTH, D, N_MOD * D), 0.5 * D ** -0.5)
    inp["ada_b"] = nrm((DEPTH, N_MOD * D), 0.02)
    inp["norm_mix"] = nrm((DEPTH, D), 0.05, 1.0)
    inp["norm_ffn"] = nrm((DEPTH, D), 0.05, 1.0)
    inp["att_w_in"] = nrm((LA, D, 3 * D_ATT + H_FOX), D ** -0.5)
    inp["att_w_o"] = nrm((LA, D_ATT, D), D_ATT ** -0.5)
    inp["att_q_gain"] = nrm((LA, HEAD_DIM), 0.05, 1.0)
    inp["att_k_gain"] = nrm((LA, HEAD_DIM), 0.05, 1.0)
    inp["att_f_bias"] = nrm((LA, H_FOX), 0.5, 3.0)
    inp["rw_mix"] = jax.random.uniform(nk(), (LR, 6, D), jnp.float32)
    inp["rw_w0"] = nrm((LR, D), 1.0, 1.0)
    inp["rw_w1"] = nrm((LR, D, R_DECAY), D ** -0.5)
    inp["rw_w2"] = nrm((LR, R_DECAY, D), 0.5 * R_DECAY ** -0.5)
    inp["rw_a0"] = nrm((LR, D), 0.5)
    inp["rw_a1"] = nrm((LR, D, R_AAA), D ** -0.5)
    inp["rw_a2"] = nrm((LR, R_AAA, D), 0.5 * R_AAA ** -0.5)
    inp["rw_g1"] = nrm((LR, D, R_GATE), D ** -0.5)
    inp["rw_g2"] = nrm((LR, R_GATE, D), R_GATE ** -0.5)
    inp["rw_k_k"] = nrm((LR, D), 0.1, 0.85)
    inp["rw_k_a"] = nrm((LR, D), 0.1, 1.0)
    inp["rw_r_k"] = nrm((LR, H_RWKV, RWKV_HS), 0.1)
    inp["rw_w_r"] = nrm((LR, D, D), D ** -0.5)
    inp["rw_w_k"] = nrm((LR, D, D), D ** -0.5)
    inp["rw_w_v"] = nrm((LR, D, D), D ** -0.5)
    inp["rw_w_o"] = nrm((LR, D, D), D ** -0.5)
    inp["rw_ln_w"] = nrm((LR, D), 0.05, 1.0)
    inp["rw_ln_b"] = nrm((LR, D), 0.02)
    inp["peer_wq"] = nrm((DEPTH, D, PEER_HEADS * D_KEY), D ** -0.5)
    inp["peer_k1"] = nrm((DEPTH, PEER_HEADS, N_KEYS, D_KEY // 2), (D_KEY // 2) ** -0.5)
    inp["peer_k2"] = nrm((DEPTH, PEER_HEADS, N_KEYS, D_KEY // 2), (D_KEY // 2) ** -0.5)
    inp["peer_u"] = nrm((DEPTH, N_EXPERTS, D), D ** -0.5)
    inp["peer_v"] = nrm((DEPTH, N_EXPERTS, D), PEER_HEADS ** -0.5)
    return inp


def reference(x_prompt, x_sample, cache_k, cache_v, cache_logf, state_wkv, state_shift, page_table,
              c_prompt, c_sample, ada_w, ada_b, norm_mix, norm_ffn,
              att_w_in, att_w_o, att_q_gain, att_k_gain, att_f_bias,
              rw_mix, rw_w0, rw_w1, rw_w2, rw_a0, rw_a1, rw_a2, rw_g1, rw_g2, rw_k_k, rw_k_a, rw_r_k,
              rw_w_r, rw_w_k, rw_w_v, rw_w_o, rw_ln_w, rw_ln_b,
              peer_wq, peer_k1, peer_k2, peer_u, peer_v):
    xp, xs = x_prompt, x_sample
    Bp = xp.shape[0]
    kp_l, vp_l, lfp_l, ks_l, vs_l, lfs_l = [], [], [], [], [], []
    wkvp_l, shp_l, wkvs_l, shs_l = [], [], [], []
    for l in range(DEPTH):
        mp = ada_mod(c_prompt, ada_w[l], ada_b[l])
        ms = ada_mod(c_sample, ada_w[l], ada_b[l])
        hp = modulate(rmsnorm(xp, norm_mix[l]), mp[:, 0], mp[:, 1])
        hs = modulate(rmsnorm(xs, norm_mix[l]), ms[:, 0], ms[:, 1])
        i = l // 2
        if l % 2 == 0:
            att = (att_w_in[i], att_q_gain[i], att_k_gain[i], att_f_bias[i], att_w_o[i])
            yp, k_new, v_new, lf_new = attn_prompt(hp, *att)
            ys, k_new2, v_new2, lf_new2 = attn_sample(hs, cache_k, cache_v, cache_logf, i, page_table, *att)
            kp_l.append(k_new); vp_l.append(v_new); lfp_l.append(lf_new.astype(cache_logf.dtype))
            ks_l.append(k_new2); vs_l.append(v_new2); lfs_l.append(lf_new2.astype(cache_logf.dtype))
        else:
            rw = (rw_mix[i], rw_w0[i], rw_w1[i], rw_w2[i], rw_a0[i], rw_a1[i], rw_a2[i], rw_g1[i], rw_g2[i],
                  rw_k_k[i], rw_k_a[i], rw_r_k[i], rw_w_r[i], rw_w_k[i], rw_w_v[i], rw_w_o[i],
                  rw_ln_w[i], rw_ln_b[i])
            yp, sh_new, wkv_new = rwkv7(hp, jnp.zeros((Bp, D_MODEL), hp.dtype),
                                        jnp.zeros((Bp, H_RWKV, RWKV_HS, RWKV_HS), jnp.float32), *rw)
            ys, sh_new2, wkv_new2 = rwkv7(hs, state_shift[i], state_wkv[i], *rw)
            wkvp_l.append(wkv_new.astype(state_wkv.dtype)); shp_l.append(sh_new.astype(state_shift.dtype))
            wkvs_l.append(wkv_new2.astype(state_wkv.dtype)); shs_l.append(sh_new2.astype(state_shift.dtype))
        xp = xp + mp[:, 2][:, None, :] * yp
        xs = xs + ms[:, 2][:, None, :] * ys
        peer_w = (peer_wq[l], peer_k1[l], peer_k2[l], peer_u[l], peer_v[l])
        hp = modulate(rmsnorm(xp, norm_ffn[l]), mp[:, 3], mp[:, 4])
        hs = modulate(rmsnorm(xs, norm_ffn[l]), ms[:, 3], ms[:, 4])
        xp = xp + mp[:, 5][:, None, :] * peer(hp, *peer_w)
        xs = xs + ms[:, 5][:, None, :] * peer(hs, *peer_w)
    return (xp, xs,
            jnp.stack(kp_l), jnp.stack(vp_l), jnp.stack(lfp_l), jnp.stack(wkvp_l), jnp.stack(shp_l),
            jnp.stack(ks_l), jnp.stack(vs_l), jnp.stack(lfs_l), jnp.stack(wkvs_l), jnp.stack(shs_l))
```

```python
import functools

import jax
import jax.numpy as jnp
from jax import lax
from jax.experimental import pallas as pl
from jax.experimental.pallas import tpu as pltpu

F32 = jnp.float32
BF16 = jnp.bfloat16

HEAD_DIM = 128
H_FOX = 8
H_ATT = 16
ATT_SCALE = HEAD_DIM ** -0.5
NEG_INF = -1e30
RWKV_HS = 64
LNX_EPS = 64e-5
PEER_HEADS = 8
PEER_TOPK = 16
N_KEYS = 128
N_MOD = 6
RMS_EPS = 1e-6

LANES = 128
SUBLANES = 8
VMEM_LIMIT_BYTES = 56 * 1024 * 1024


def _cparams(sem):
    return pltpu.CompilerParams(dimension_semantics=sem, vmem_limit_bytes=VMEM_LIMIT_BYTES)


def _split_bf16(x, parts):
    out = []
    for _ in range(parts - 1):
        hi = x.astype(BF16)
        out.append(hi)
        x = x - hi.astype(F32)
    out.append(x.astype(BF16))
    return out


def _dot_exact_rhs(x, rhs_bf16, parts=3):
    acc = None
    for p in _split_bf16(x, parts):
        t = jnp.dot(p, rhs_bf16, preferred_element_type=F32)
        acc = t if acc is None else acc + t
    return acc


def _dot_exact_lhs(lhs_bf16, x, parts=3):
    acc = None
    for p in _split_bf16(x, parts):
        t = jnp.dot(lhs_bf16, p, preferred_element_type=F32)
        acc = t if acc is None else acc + t
    return acc


def _dotf(a, b):
    a3 = _split_bf16(a, 3)
    b3 = _split_bf16(b, 3)
    acc = None
    for i, j in ((2, 0), (0, 2), (1, 1), (1, 0), (0, 1), (0, 0)):
        t = jnp.dot(a3[i], b3[j], preferred_element_type=F32)
        acc = t if acc is None else acc + t
    return acc


def _dotf_nt(a, b):
    a3 = _split_bf16(a, 3)
    b3 = _split_bf16(b, 3)
    acc = None
    dn = (((1,), (1,)), ((), ()))
    for i, j in ((2, 0), (0, 2), (1, 1), (1, 0), (0, 1), (0, 0)):
        t = lax.dot_general(a3[i], b3[j], dn, preferred_element_type=F32)
        acc = t if acc is None else acc + t
    return acc


def _softplus(z):
    return jnp.maximum(z, 0.0) + jnp.log1p(jnp.exp(-jnp.abs(z)))


def _sigmoid(z):
    return 1.0 / (1.0 + jnp.exp(-z))


class _Group:
    def __init__(self, batch, seq, tm):
        self.batch, self.seq = batch, seq
        self.m = batch * seq
        if seq == 1:
            self.tm = min(tm, self.m)
        else:
            self.tm = min(tm, seq)
        assert self.m % self.tm == 0 and (seq == 1 or seq % self.tm == 0)
        self.tiles_per_seq = max(seq // self.tm, 1)

    def seq_vec(self, v):
        return v if self.seq == 1 else v[:, None, :]

    def seq_spec(self, width, col=None):
        col = col or (lambda *ids: 0)
        if self.seq == 1:
            return pl.BlockSpec((self.tm, width), lambda *ids: (ids[0], col(*ids)))
        tps = self.tiles_per_seq
        return pl.BlockSpec((None, 1, width), lambda *ids: (ids[0] // tps, 0, col(*ids)))


def _normmod_body(x_ref, g_ref, shift_ref, scale_ref, o_ref):
    x = x_ref[...]
    h = x * lax.rsqrt(jnp.mean(x * x, axis=-1, keepdims=True) + RMS_EPS) * g_ref[...]
    o_ref[...] = h * (1.0 + scale_ref[...]) + shift_ref[...]


def _normmod(grp, x, g, shift, scale):
    m, d = x.shape
    return pl.pallas_call(
        _normmod_body,
        grid=(m // grp.tm,),
        in_specs=[pl.BlockSpec((grp.tm, d), lambda i: (i, 0)),
                  pl.BlockSpec((1, d), lambda i: (0, 0)),
                  grp.seq_spec(d), grp.seq_spec(d)],
        out_specs=pl.BlockSpec((grp.tm, d), lambda i: (i, 0)),
        out_shape=jax.ShapeDtypeStruct((m, d), F32),
        compiler_params=_cparams(("parallel",)),
        name="normmod",
    )(x, g[None, :], grp.seq_vec(shift), grp.seq_vec(scale))


def _mm_body(*refs, n_a, n_e, n_out, prologue, epilogue, every_j):
    a_refs = refs[:n_a]
    w_ref = refs[n_a]
    e_refs = refs[n_a + 1:n_a + 1 + n_e]
    o_refs = refs[n_a + 1 + n_e:n_a + 1 + n_e + n_out]
    a_scr = refs[-1]
    j = pl.program_id(1)

    def fill():
        a_scr[...] = prologue(*[r[...] for r in a_refs]).astype(BF16)

    if every_j:
        fill()
    else:
        pl.when(j == 0)(fill)
    acc = jnp.dot(a_scr[...], w_ref[...].astype(BF16), preferred_element_type=F32)
    epilogue(acc, j, e_refs, o_refs)


def _mm(m, tm, w, tn, a_ins, a_specs, e_ins, e_specs, out_shapes, out_specs, prologue, epilogue, name,
        every_j=False):
    k, n = w.shape
    assert m % tm == 0 and n % tn == 0
    body = functools.partial(_mm_body, n_a=len(a_ins), n_e=len(e_ins), n_out=len(out_shapes),
                             prologue=prologue, epilogue=epilogue, every_j=every_j)
    return pl.pallas_call(
        body,
        grid=(m // tm, n // tn),
        in_specs=list(a_specs) + [pl.BlockSpec((k, tn), lambda i, j: (0, j))] + list(e_specs),
        out_specs=list(out_specs),
        out_shape=list(out_shapes),
        scratch_shapes=[pltpu.VMEM((tm, k), BF16)],
        compiler_params=_cparams(("parallel", "arbitrary")),
        name=name,
    )(*a_ins, w, *e_ins)


def _tile_spec(tm, width):
    return pl.BlockSpec((tm, width), lambda i, j: (i, 0))


def _out_spec(tm, tn):
    return pl.BlockSpec((tm, tn), lambda i, j: (i, j))


def _row_spec(tn):
    return pl.BlockSpec((1, tn), lambda i, j: (0, j))


def _ident(a):
    return a


def _epi(fn):
    def run(acc, j, e_refs, o_refs):
        o_refs[0][...] = fn(acc, j, *[r[...] for r in e_refs])
    return run


def _ada_mod(c, w, b):
    m, d = c.shape
    n = w.shape[1]
    tn = 512
    (out,) = _mm(m, m, w, tn, [c], [_tile_spec(m, d)], [b[None, :]], [_row_spec(tn)],
                 [jax.ShapeDtypeStruct((m, n), F32)], [_out_spec(m, tn)],
                 lambda cc: cc * _sigmoid(cc), _epi(lambda acc, j, bb: acc + bb), "ada_mod")
    return out


def _head_rmsnorm(y, gain):
    blocks = []
    for hh in range(y.shape[1] // HEAD_DIM):
        blk = y[:, hh * HEAD_DIM:(hh + 1) * HEAD_DIM]
        ms = jnp.mean(blk * blk, axis=-1, keepdims=True)
        blocks.append(blk * lax.rsqrt(ms + RMS_EPS) * gain)
    return jnp.concatenate(blocks, axis=1)


def _qkv_epilogue(acc, j, e_refs, o_refs):
    gains = e_refs[0]
    q_ref, k_ref, v_ref = o_refs

    @pl.when(j == 0)
    def _():
        q_ref[...] = _head_rmsnorm(acc, gains[0:1, :])

    @pl.when(j == 1)
    def _():
        q_ref[...] = acc

    @pl.when(j == 2)
    def _():
        k_ref[...] = _head_rmsnorm(acc, gains[1:2, :])

    @pl.when(j == 3)
    def _():
        k_ref[...] = acc

    @pl.when(j >= 4)
    def _():
        v_ref[...] = acc


def _att_project(grp, h, w_qkv, w_f, q_gain, k_gain, f_bias):
    m, d = h.shape
    tm = grp.tm
    d_att = H_ATT * HEAD_DIM
    tn = d_att // 2
    gains = jnp.stack([q_gain, k_gain])
    shp = jax.ShapeDtypeStruct((m, d_att), F32)

    def ospec(first):
        return pl.BlockSpec((tm, tn), lambda i, j: (i, jnp.clip(j - first, 0, 1)))

    q, k, v = _mm(m, tm, w_qkv, tn, [h], [_tile_spec(tm, d)],
                  [gains], [pl.BlockSpec((2, HEAD_DIM), lambda i, j: (0, 0))],
                  [shp, shp, shp], [ospec(0), ospec(2), ospec(4)],
                  _ident, _qkv_epilogue, "att_qkv")
    (logf,) = _mm(m, tm, w_f, LANES, [h], [_tile_spec(tm, d)],
                  [f_bias], [_row_spec(LANES)],
                  [jax.ShapeDtypeStruct((m, LANES), F32)], [_out_spec(tm, LANES)],
                  _ident, _epi(lambda acc, j, fb: -_softplus(-(acc + fb))), "att_logf")
    return q, k, v, logf


def _cumsum_body(x_ref, o_ref):
    t = x_ref.shape[0]
    row = lax.broadcasted_iota(jnp.int32, (t, t), 0)
    col = lax.broadcasted_iota(jnp.int32, (t, t), 1)
    tri = (col <= row).astype(BF16)
    o_ref[...] = _dot_exact_lhs(tri, x_ref[...])


def _cumsum_seq(x):
    b, t, w = x.shape
    return pl.pallas_call(
        _cumsum_body,
        grid=(b,),
        in_specs=[pl.BlockSpec((None, t, w), lambda i: (i, 0, 0))],
        out_specs=pl.BlockSpec((None, t, w), lambda i: (i, 0, 0)),
        out_shape=jax.ShapeDtypeStruct((b, t, w), F32),
        compiler_params=_cparams(("parallel",)),
        name="fox_cumsum",
    )(x)


def _attn_prompt_body(q_ref, k_ref, v_ref, fc_ref, ft_ref, o_ref, *, tq):
    h = pl.program_id(1)
    qi = pl.program_id(2)
    q = q_ref[...].astype(BF16)
    t_pos = qi * tq + lax.broadcasted_iota(jnp.int32, (tq, tq), 0)
    s_off = lax.broadcasted_iota(jnp.int32, (tq, tq), 1)
    nt = (((1,), (1,)), ((), ()))

    def scores(kb):
        start = pl.multiple_of(kb * tq, tq)
        kblk = k_ref[pl.ds(start, tq), :].astype(BF16)
        vblk = v_ref[pl.ds(start, tq), :].astype(BF16)
        s = lax.dot_general(q, kblk, nt, preferred_element_type=F32) * ATT_SCALE
        return s, vblk, start

    @pl.when(h < H_FOX)
    def _fox():
        lane = lax.broadcasted_iota(jnp.int32, fc_ref.shape, 1)
        fq = jnp.sum(jnp.where(lane == h, fc_ref[...], 0.0), axis=-1, keepdims=True)

        def step(kb, carry):
            m_run, l_run, acc = carry
            s, vblk, start = scores(kb)
            fk = ft_ref[pl.ds(h, 1), pl.ds(start, tq)]
            s = s + (fq - fk)
            s = jnp.where(kb * tq + s_off <= t_pos, s, NEG_INF)
            m_new = jnp.maximum(m_run, jnp.max(s, axis=-1, keepdims=True))
            alpha = jnp.exp(m_run - m_new)
            p = jnp.exp(s - m_new)
            l_new = alpha * l_run + jnp.sum(p, axis=-1, keepdims=True)
            acc = alpha * acc + jnp.dot(p.astype(BF16), vblk, preferred_element_type=F32)
            return m_new, l_new, acc

        init = (jnp.full((tq, 1), NEG_INF, F32), jnp.zeros((tq, 1), F32), jnp.zeros((tq, HEAD_DIM), F32))
        _, l_fin, acc = lax.fori_loop(0, qi + 1, step, init)
        o_ref[...] = acc / l_fin

    @pl.when(h >= H_FOX)
    def _sb():
        jj = lax.broadcasted_iota(jnp.int32, (tq, tq), 0)
        ss = lax.broadcasted_iota(jnp.int32, (tq, tq), 1)
        upper = (jj > ss).astype(BF16)

        def step(it, carry):
            later, acc = carry
            kb = qi - it
            z, vblk, _ = scores(kb)
            strict = kb * tq + s_off < t_pos
            lk = jnp.where(strict, -_softplus(z), 0.0)
            later_in = _dot_exact_rhs(lk, upper, parts=2)
            a = jnp.where(strict, jnp.exp(z + lk + later_in + later), 0.0)
            acc = acc + jnp.dot(a.astype(BF16), vblk, preferred_element_type=F32)
            later = later + jnp.sum(lk, axis=-1, keepdims=True)
            return later, acc

        init = (jnp.zeros((tq, 1), F32), jnp.zeros((tq, HEAD_DIM), F32))
        _, acc = lax.fori_loop(0, qi + 1, step, init)
        o_ref[...] = acc


def _attn_prompt(q, k, v, fcum, fcum_t, batch, seq):
    d_att = q.shape[1]
    tq = min(256, seq)
    q3, k3, v3 = (a.reshape(batch, seq, d_att) for a in (q, k, v))
    body = functools.partial(_attn_prompt_body, tq=tq)
    out = pl.pallas_call(
        body,
        grid=(batch, H_ATT, seq // tq),
        in_specs=[pl.BlockSpec((None, tq, HEAD_DIM), lambda b, h, i: (b, i, h)),
                  pl.BlockSpec((None, seq, HEAD_DIM), lambda b, h, i: (b, 0, h)),
                  pl.BlockSpec((None, seq, HEAD_DIM), lambda b, h, i: (b, 0, h)),
                  pl.BlockSpec((None, tq, LANES), lambda b, h, i: (b, i, 0)),
                  pl.BlockSpec((None, H_FOX, seq), lambda b, h, i: (b, 0, 0))],
        out_specs=pl.BlockSpec((None, tq, HEAD_DIM), lambda b, h, i: (b, i, h)),
        out_shape=jax.ShapeDtypeStruct((batch, seq, d_att), F32),
        compiler_params=_cparams(("parallel", "parallel", "arbitrary")),
        name="attn_prompt",
    )(q3, k3, v3, fcum, fcum_t)
    return out.reshape(batch * seq, d_att)


def _attn_sample_body(pt_ref, q_ref, kn_ref, vn_ref, lfn_ref, kc_ref, vc_ref, lfc_ref, o_ref,
                      qrows, m_scr, l_scr, suf_scr, acc_scr, *, n_pages):
    p = pl.program_id(1)
    d_att = q_ref.shape[-1]
    page = kc_ref.shape[0]
    sub = lax.broadcasted_iota(jnp.int32, (H_ATT, d_att), 0)
    lane_head = lax.broadcasted_iota(jnp.int32, (H_ATT, d_att), 1) // HEAD_DIM
    own = sub == lane_head
    is_fox = lax.broadcasted_iota(jnp.int32, (H_ATT, 1), 0) < H_FOX
    nt = (((1,), (1,)), ((), ()))

    @pl.when(p == 0)
    def _init():
        qr = jnp.where(own, q_ref[...], 0.0)
        qrows[...] = qr.astype(BF16)
        s_self = jnp.sum(qr * kn_ref[...], axis=-1, keepdims=True) * ATT_SCALE
        m_scr[...] = jnp.where(is_fox, s_self, 0.0)
        l_scr[...] = jnp.ones_like(l_scr)
        acc_scr[...] = jnp.where(sub < H_FOX, jnp.broadcast_to(vn_ref[...], (H_ATT, d_att)), 0.0)
        lf_lane = lax.broadcasted_iota(jnp.int32, (H_ATT, LANES), 1)
        lf_sub = lax.broadcasted_iota(jnp.int32, (H_ATT, LANES), 0)
        lf_new = jnp.sum(jnp.where(lf_lane == lf_sub, lfn_ref[...], 0.0), axis=-1, keepdims=True)
        suf_scr[...] = jnp.where(is_fox, lf_new, 0.0)

    s = lax.dot_general(qrows[...], kc_ref[...].astype(BF16), nt,
                        preferred_element_type=F32) * ATT_SCALE
    jj = lax.broadcasted_iota(jnp.int32, (page, page), 0)
    ss = lax.broadcasted_iota(jnp.int32, (page, page), 1)
    upper = (jj > ss).astype(BF16)
    lk = -_softplus(s)
    lf = jnp.concatenate([lfc_ref[...], jnp.zeros((H_ATT - H_FOX, page), F32)], axis=0)
    step_log = jnp.where(is_fox, lf, lk)
    later = _dot_exact_rhs(step_log, upper) + suf_scr[...]
    logit = s + later + jnp.where(is_fox, 0.0, lk)
    m_old = m_scr[...]
    m_new = jnp.where(is_fox, jnp.maximum(m_old, jnp.max(logit, axis=-1, keepdims=True)), 0.0)
    alpha = jnp.exp(m_old - m_new)
    w = jnp.exp(logit - m_new)
    l_scr[...] = jnp.where(is_fox, alpha * l_scr[...] + jnp.sum(w, axis=-1, keepdims=True), 1.0)
    acc_scr[...] = alpha * acc_scr[...] + jnp.dot(w.astype(BF16), vc_ref[...].astype(BF16),
                                                  preferred_element_type=F32)
    m_scr[...] = m_new
    suf_scr[...] = suf_scr[...] + jnp.sum(step_log, axis=-1, keepdims=True)

    @pl.when(p == n_pages - 1)
    def _fin():
        full = acc_scr[...] / l_scr[...]
        o_ref[...] = jnp.sum(jnp.where(own, full, 0.0), axis=0, keepdims=True)


def _attn_sample(q, k_new, v_new, logf_new, cache_k, cache_v, cache_lf_t, page_table):
    b, d_att = q.shape
    n_pages = page_table.shape[1]
    page = cache_k.shape[1]
    row = lambda a: a[:, None, :]

    def new_spec(width):
        return pl.BlockSpec((None, 1, width), lambda i, p, pt: (i, 0, 0))

    def page_spec(shape):
        return pl.BlockSpec((None,) + shape, lambda i, p, pt: (pt[i, n_pages - 1 - p], 0, 0))

    grid_spec = pltpu.PrefetchScalarGridSpec(
        num_scalar_prefetch=1,
        grid=(b, n_pages),
        in_specs=[new_spec(d_att), new_spec(d_att), new_spec(d_att), new_spec(LANES),
                  page_spec((page, d_att)), page_spec((page, d_att)), page_spec((H_FOX, page))],
        out_specs=pl.BlockSpec((None, 1, d_att), lambda i, p, pt: (i, 0, 0)),
        scratch_shapes=[pltpu.VMEM((H_ATT, d_att), BF16), pltpu.VMEM((H_ATT, 1), F32),
                        pltpu.VMEM((H_ATT, 1), F32), pltpu.VMEM((H_ATT, 1), F32),
                        pltpu.VMEM((H_ATT, d_att), F32)],
    )
    out = pl.pallas_call(
        functools.partial(_attn_sample_body, n_pages=n_pages),
        grid_spec=grid_spec,
        out_shape=jax.ShapeDtypeStruct((b, 1, d_att), F32),
        compiler_params=_cparams(("parallel", "arbitrary")),
        name="attn_sample",
    )(page_table, row(q), row(k_new), row(v_new), row(logf_new), cache_k, cache_v, cache_lf_t)
    return out.reshape(b, d_att)


def _proj_residual(grp, a, w, x, gate, name):
    m, d = x.shape
    tm, tn = grp.tm, min(1024, d)
    (out,) = _mm(m, tm, w, tn, [a], [_tile_spec(tm, a.shape[1])],
                 [x, grp.seq_vec(gate)], [_out_spec(tm, tn), grp.seq_spec(tn, col=lambda i, j: j)],
                 [jax.ShapeDtypeStruct((m, d), F32)], [_out_spec(tm, tn)],
                 _ident, _epi(lambda acc, j, xx, gg: xx + gg * acc), name)
    return out


def _mix_prologue(h, prev, mix):
    return h + (prev - h) * mix


def _rwkv_project(grp, h, prev, mix, w_rkv, w_lr1, w_lr2, bias_lr2):
    m, d = h.shape
    tm = grp.tm
    tn = d // 2
    mix_rkv = jnp.stack([mix[0], mix[2], mix[3]])[:, None, :]
    (rkv,) = _mm(m, tm, w_rkv, tn, [h, prev, mix_rkv],
                 [_tile_spec(tm, d), _tile_spec(tm, d),
                  pl.BlockSpec((None, 1, d), lambda i, j: (j // 2, 0, 0))],
                 [], [], [jax.ShapeDtypeStruct((m, 3 * d), F32)], [_out_spec(tm, tn)],
                 _mix_prologue, _epi(lambda acc, j: acc), "rwkv_rkv", every_j=True)

    mix_lr = jnp.stack([mix[1], mix[4], mix[5], mix[5]])[:, None, :]

    def lr1_act(acc, j):
        return lax.cond(j == 0, jnp.tanh, lambda a: lax.cond(j == 1, _ident, _sigmoid, a), acc)

    n1 = w_lr1.shape[1]
    (lr1,) = _mm(m, tm, w_lr1, LANES, [h, prev, mix_lr],
                 [_tile_spec(tm, d), _tile_spec(tm, d),
                  pl.BlockSpec((None, 1, d), lambda i, j: (j, 0, 0))],
                 [], [], [jax.ShapeDtypeStruct((m, n1), F32)], [_out_spec(tm, LANES)],
                 _mix_prologue, _epi(lr1_act), "rwkv_lr1", every_j=True)

    def lr2_act(acc, j, bias):
        y = acc + bias
        decay = lambda a: -jnp.exp(-_softplus(-a) - 0.5)
        return lax.cond(j < 2, decay, lambda a: lax.cond(j < 4, _sigmoid, _ident, a), y)

    (lag,) = _mm(m, tm, w_lr2, tn, [lr1], [_tile_spec(tm, n1)],
                 [bias_lr2], [_row_spec(tn)],
                 [jax.ShapeDtypeStruct((m, 3 * d), F32)], [_out_spec(tm, tn)],
                 _ident, _epi(lr2_act), "rwkv_lr2")
    return rkv, lag


def _dot3(a, b, dn):
    a_hi, a_lo = _split_bf16(a, 2)
    b_hi, b_lo = _split_bf16(b, 2)
    return (lax.dot_general(a_hi, b_hi, dn, preferred_element_type=F32)
            + lax.dot_general(a_hi, b_lo, dn, preferred_element_type=F32)
            + lax.dot_general(a_lo, b_hi, dn, preferred_element_type=F32))


_NN = (((1,), (0,)), ((), ()))
_NT = (((1,), (1,)), ((), ()))


def _half_sum(x, lo_half):
    s0 = jnp.sum(jnp.where(lo_half, x, 0.0), axis=-1, keepdims=True)
    s1 = jnp.sum(jnp.where(lo_half, 0.0, x), axis=-1, keepdims=True)
    return jnp.where(lo_half, s0, s1)


def _rwkv_scan_body(r_ref, k_ref, v_ref, lw_ref, a_ref, g_ref, par_ref, s0_ref, z_ref, sT_ref, s_scr,
                    *, chunk, n_chunks):
    hs = RWKV_HS
    lane = lax.broadcasted_iota(jnp.int32, (1, 2 * hs), 1)
    lo_half = lane < hs
    k_k, k_a, r_k, ln_w, ln_b = (par_ref[i:i + 1, :] for i in range(5))
    zero = jnp.zeros((hs, hs), F32)
    s_scr[...] = jnp.concatenate([jnp.concatenate([s0_ref[0], zero], axis=1),
                                  jnp.concatenate([zero, s0_ref[1]], axis=1)], axis=0)
    row = lax.broadcasted_iota(jnp.int32, (chunk, chunk), 0)
    col = lax.broadcasted_iota(jnp.int32, (chunk, chunk), 1)
    tril_incl = (col <= row)
    tril_strict = (col < row)
    eye = (col == row).astype(F32)
    rr = lax.broadcasted_iota(jnp.int32, (2 * hs, 2 * hs), 0) < hs
    cc = lax.broadcasted_iota(jnp.int32, (2 * hs, 2 * hs), 1) < hs
    same_head = rr == cc
    n_double = max(chunk.bit_length() - 2, 0)

    def one_chunk(c, _):
        t0 = pl.multiple_of(c * chunk, chunk)
        sl = pl.ds(t0, chunk)
        r, k, v, lw, a, g = (ref[sl, :] for ref in (r_ref, k_ref, v_ref, lw_ref, a_ref, g_ref))
        kk = k * k_k
        kk = kk / jnp.maximum(jnp.sqrt(_half_sum(kk * kk, lo_half)), 1e-12)
        kmod = k * (1.0 + (a - 1.0) * k_a)
        bb = kk * a
        cum = _dot_exact_lhs(tril_incl.astype(BF16), lw)
        dec_in, dec_ex = jnp.exp(cum), jnp.exp(cum - lw)
        inv = jnp.exp(-cum)
        a_hat, b_hat, k_hat, r_hat = -kk * dec_ex, bb * inv, kmod * inv, r * dec_in
        state = s_scr[...]

        def per_head(fn):
            return jnp.where(lo_half, fn(lo_half), fn(jnp.logical_not(lo_half)))

        def solve(msk):
            am = jnp.where(msk, a_hat, 0.0)
            n_mat = jnp.where(tril_strict, _dot3(am, b_hat, _NT), 0.0)
            m_mat = jnp.where(tril_strict, _dot3(am, k_hat, _NT), 0.0)
            rhs = _dot3(a_hat, state, _NT) + _dot3(m_mat, v, _NN)
            inv_mat, pw = eye + n_mat, n_mat
            for _ in range(n_double):
                pw = _dot3(pw, pw, _NN)
                inv_mat = inv_mat + _dot3(inv_mat, pw, _NN)
            return _dot3(inv_mat, rhs, _NN)

        u = per_head(solve)

        def readout(msk):
            rm = jnp.where(msk, r_hat, 0.0)
            no = jnp.where(tril_incl, _dot3(rm, b_hat, _NT), 0.0)
            mo = jnp.where(tril_incl, _dot3(rm, k_hat, _NT), 0.0)
            return _dot3(no, u, _NN) + _dot3(mo, v, _NN)

        o = _dot3(r_hat, state, _NT) + per_head(readout)
        last = dec_in[chunk - 1:chunk, :]
        upd = _dot3(u.T, b_hat * last, _NN) + _dot3(v.T, k_hat * last, _NN)
        s_scr[...] = jnp.where(same_head, state * last + upd, 0.0)

        mu = _half_sum(o, lo_half) / hs
        dev = o - mu
        var = _half_sum(dev * dev, lo_half) / hs
        o_n = dev * lax.rsqrt(var + LNX_EPS) * ln_w + ln_b
        bonus = _half_sum(r * kmod * r_k, lo_half) * v
        z_ref[sl, :] = (o_n + bonus) * g
        return 0

    lax.fori_loop(0, n_chunks, one_chunk, 0)
    fin = s_scr[...]
    sT_ref[0] = fin[:hs, :hs]
    sT_ref[1] = fin[hs:, hs:]


def _rwkv_scan(rkv, lag, params, state0, batch, seq, chunk):
    d = rkv.shape[-1] // 3
    pairs = d // (2 * RWKV_HS)
    blk = 2 * RWKV_HS

    def seq_spec(part):
        return pl.BlockSpec((None, seq, blk), lambda b, p: (b, 0, part * pairs + p))

    st_spec = pl.BlockSpec((None, 2, RWKV_HS, RWKV_HS), lambda b, p: (b, p, 0, 0))
    body = functools.partial(_rwkv_scan_body, chunk=chunk, n_chunks=seq // chunk)
    z, s_fin = pl.pallas_call(
        body,
        grid=(batch, pairs),
        in_specs=[seq_spec(0), seq_spec(1), seq_spec(2), seq_spec(0), seq_spec(1), seq_spec(2),
                  pl.BlockSpec((SUBLANES, blk), lambda b, p: (0, p)), st_spec],
        out_specs=[pl.BlockSpec((None, seq, blk), lambda b, p: (b, 0, p)), st_spec],
        out_shape=[jax.ShapeDtypeStruct((batch, seq, d), F32),
                   jax.ShapeDtypeStruct(state0.shape, F32)],
        scratch_shapes=[pltpu.VMEM((blk, blk), F32)],
        compiler_params=_cparams(("parallel", "parallel")),
        name="rwkv_scan",
    )(rkv, rkv, rkv, lag, lag, lag, params, state0)
    return z, s_fin


def _peer_route_body(q_ref, k1_ref, k2_ref, ids_ref, gate_ref):
    tm = q_ref.shape[0]
    topk = PEER_TOPK
    n_cand = topk * topk
    lane = lax.broadcasted_iota(jnp.int32, (tm, N_KEYS), 1)
    lane_f = lane.astype(F32)
    cl = lax.broadcasted_iota(jnp.int32, (tm, n_cand), 1)
    cl_f = cl.astype(F32)
    cl_hi, cl_lo = cl // topk, cl % topk
    half = k1_ref.shape[-1]

    def top_sub(s, slot_of_lane, id_scale):
        def it(i, carry):
            s, cv, ce = carry
            m = jnp.max(s, axis=-1, keepdims=True)
            idx = jnp.min(jnp.where(s == m, lane_f, float(N_KEYS)), axis=-1, keepdims=True)
            cv = jnp.where(slot_of_lane == i, m, cv)
            ce = jnp.where(slot_of_lane == i, idx * id_scale, ce)
            s = jnp.where(lane_f == idx, -jnp.inf, s)
            return s, cv, ce
        zero = jnp.zeros((tm, n_cand), F32)
        _, cv, ce = lax.fori_loop(0, topk, it, (s, zero, zero))
        return cv, ce

    def head(hq, carry):
        vals, ids = carry
        off = pl.multiple_of(hq * 2 * half, 2 * half)
        q1 = q_ref[:, pl.ds(off, half)].astype(BF16)
        q2 = q_ref[:, pl.ds(off + half, half)].astype(BF16)
        s1 = lax.dot_general(q1, k1_ref[hq].astype(BF16), _NT, preferred_element_type=F32)
        s2 = lax.dot_general(q2, k2_ref[hq].astype(BF16), _NT, preferred_element_type=F32)
        v1, e1 = top_sub(s1, cl_hi, float(N_KEYS))
        v2, e2 = top_sub(s2, cl_lo, 1.0)
        cs, ce = v1 + v2, e1 + e2

        def it(i, carry):
            cs, vals, ids = carry
            m = jnp.max(cs, axis=-1, keepdims=True)
            pos = jnp.min(jnp.where(cs == m, cl_f, float(n_cand)), axis=-1, keepdims=True)
            e = jnp.max(jnp.where(cl_f == pos, ce, -1.0), axis=-1, keepdims=True)
            here = lane == hq * topk + i
            vals = jnp.where(here, m, vals)
            ids = jnp.where(here, e, ids)
            cs = jnp.where(cl_f == pos, -jnp.inf, cs)
            return cs, vals, ids
        _, vals, ids = lax.fori_loop(0, topk, it, (cs, vals, ids))
        return vals, ids

    init = (jnp.full((tm, N_KEYS), -jnp.inf, F32), jnp.zeros((tm, N_KEYS), F32))
    vals, ids = lax.fori_loop(0, PEER_HEADS, head, init)
    ids_ref[...] = ids.astype(jnp.int32)

    gate = jnp.zeros((tm, N_KEYS), F32)
    for hq in range(PEER_HEADS):
        grp = (lane // topk) == hq
        mx = jnp.max(jnp.where(grp, vals, -jnp.inf), axis=-1, keepdims=True)
        ex = jnp.where(grp, jnp.exp(vals - mx), 0.0)
        gate = jnp.where(grp, ex / jnp.sum(ex, axis=-1, keepdims=True), gate)
    gate_ref[...] = gate


def _peer_route(q, k1, k2, tm):
    m, dq = q.shape
    assert PEER_HEADS * PEER_TOPK == N_KEYS
    kspec = pl.BlockSpec(k1.shape, lambda i: (0, 0, 0))
    ospec = pl.BlockSpec((tm, N_KEYS), lambda i: (i, 0))
    return pl.pallas_call(
        _peer_route_body,
        grid=(m // tm,),
        in_specs=[pl.BlockSpec((tm, dq), lambda i: (i, 0)), kspec, kspec],
        out_specs=[ospec, ospec],
        out_shape=[jax.ShapeDtypeStruct((m, N_KEYS), jnp.int32), jax.ShapeDtypeStruct((m, N_KEYS), F32)],
        compiler_params=_cparams(("parallel",)),
        name="peer_route",
    )(q, k1, k2)


def _gelu(x):
    return 0.5 * x * (1.0 + lax.erf(x * (2.0 ** -0.5)))


def _peer_expert_body(ids_ref, h_ref, gate_ref, x_ref, g5_ref, uv_hbm, o_ref, buf, sem, pout, *, tt):
    d = h_ref.shape[1]
    n_sel = gate_ref.shape[1]

    def row_copy(expert, slot, p):
        return pltpu.make_async_copy(uv_hbm.at[pl.ds(expert, 1)], buf.at[slot, pl.ds(p, 1)], sem.at[slot])

    def issue(t, slot):
        for p in range(n_sel):
            row_copy(ids_ref[t, p], slot, p).start()

    def wait_all(slot):
        pltpu.make_async_copy(uv_hbm.at[pl.ds(0, n_sel)], buf.at[slot], sem.at[slot]).wait()

    issue(0, 0)

    def token(t, _):
        slot = t % 2

        @pl.when(t + 1 < tt)
        def _():
            issue(t + 1, 1 - slot)

        wait_all(slot)
        hrow = jnp.broadcast_to(h_ref[pl.ds(t, 1), :], (SUBLANES, d)).astype(BF16)
        u_sel = buf[slot, :, :d].astype(BF16)
        act = lax.dot_general(hrow, u_sel, _NT, preferred_element_type=F32)
        wgt = _gelu(act) * gate_ref[pl.ds(t, 1), :]
        v_sel = buf[slot, :, d:].astype(BF16)
        out = jnp.dot(wgt.astype(BF16), v_sel, preferred_element_type=F32)
        pout[pl.ds(t, 1), :] = out[0:1, :]
        return 0

    lax.fori_loop(0, tt, token, 0)
    o_ref[...] = x_ref[...] + g5_ref[...] * pout[...]


def _peer_experts(grp, h, ids, gates, x, gate5, uv, tt):
    m, d = h.shape
    n_sel = ids.shape[1]
    tt = min(tt, grp.tm)
    assert grp.tm % tt == 0
    sub = grp.tm // tt
    tile = lambda width: pl.BlockSpec((tt, width), lambda i: (i, 0))
    if grp.seq == 1:
        g5, g5_spec = gate5, tile(d)
    else:
        tps = grp.tiles_per_seq * sub
        g5, g5_spec = gate5[:, None, :], pl.BlockSpec((None, 1, d), lambda i: (i // tps, 0, 0))
    return pl.pallas_call(
        functools.partial(_peer_expert_body, tt=tt),
        grid=(m // tt,),
        in_specs=[pl.BlockSpec((tt, n_sel), lambda i: (i, 0), memory_space=pltpu.SMEM),
                  tile(d), tile(n_sel), tile(d), g5_spec,
                  pl.BlockSpec(memory_space=pl.ANY)],
        out_specs=tile(d),
        out_shape=jax.ShapeDtypeStruct((m, d), F32),
        scratch_shapes=[pltpu.VMEM((2, n_sel, 2 * d), F32), pltpu.SemaphoreType.DMA((2,)),
                        pltpu.VMEM((tt, d), F32)],
        compiler_params=_cparams(("arbitrary",)),
        name="peer_experts",
    )(ids, h, gates, x, g5, uv)


def _peer(grp, x, mod, norm_g, w_q, k1, k2, uv):
    h = _normmod(grp, x, norm_g, mod[:, 3], mod[:, 4])
    m, d = h.shape
    tn = min(1024, w_q.shape[1])
    (q,) = _mm(m, grp.tm, w_q, tn, [h], [_tile_spec(grp.tm, d)], [], [],
               [jax.ShapeDtypeStruct((m, w_q.shape[1]), F32)], [_out_spec(grp.tm, tn)],
               _ident, _epi(lambda acc, j: acc), "peer_q")
    ids, gates = _peer_route(q, k1, k2, min(128, m))
    return _peer_experts(grp, h, ids, gates, x, mod[:, 5], uv, 32)


def _pad_cols(w, width):
    return jnp.pad(w, ((0, 0), (0, width - w.shape[1])))


def _pad_rows(w, height):
    return jnp.pad(w, ((0, height - w.shape[0]), (0, 0)))


def kernel(x_prompt, x_sample, cache_k, cache_v, cache_logf, state_wkv, state_shift, page_table, c_prompt, c_sample, ada_w, ada_b, norm_mix, norm_ffn, att_w_in, att_w_o, att_q_gain, att_k_gain, att_f_bias, rw_mix, rw_w0, rw_w1, rw_w2, rw_a0, rw_a1, rw_a2, rw_g1, rw_g2, rw_k_k, rw_k_a, rw_r_k, rw_w_r, rw_w_k, rw_w_v, rw_w_o, rw_ln_w, rw_ln_b, peer_wq, peer_k1, peer_k2, peer_u, peer_v):
    bp, tp, d = x_prompt.shape
    bs = x_sample.shape[0]
    depth = ada_w.shape[0]
    d_att = H_ATT * HEAD_DIM
    groups = (_Group(bp, tp, 512), _Group(bs, 1, 128))
    xs = [x_prompt.reshape(bp * tp, d), x_sample.reshape(bs, d)]

    c_all = jnp.concatenate([c_prompt, c_sample], axis=0)
    n_seq = c_all.shape[0]
    c_all = jnp.pad(c_all, ((0, (-n_seq) % SUBLANES), (0, 0)))

    outs = {name: [[], []] for name in ("k", "v", "lf", "wkv", "shift")}
    for l in range(depth):
        mod_all = _ada_mod(c_all, ada_w[l], ada_b[l])[:n_seq].reshape(n_seq, N_MOD, d)
        mods = (mod_all[:bp], mod_all[bp:])
        i = l // 2
        if l % 2 == 0:
            w_qkv = att_w_in[i][:, :3 * d_att].astype(BF16)
            w_f = _pad_cols(att_w_in[i][:, 3 * d_att:], LANES).astype(BF16)
            f_bias = _pad_cols(att_f_bias[i][None, :], LANES)
            w_o = att_w_o[i].astype(BF16)
            for gi, grp in enumerate(groups):
                x, mod = xs[gi], mods[gi]
                h = _normmod(grp, x, norm_mix[l], mod[:, 0], mod[:, 1])
                q, k, v, logf = _att_project(grp, h, w_qkv, w_f, att_q_gain[i], att_k_gain[i], f_bias)
                if grp.seq > 1:
                    fcum = _cumsum_seq(logf.reshape(grp.batch, grp.seq, LANES))
                    fcum_t = jnp.swapaxes(fcum[:, :, :H_FOX], 1, 2)
                    o = _attn_prompt(q, k, v, fcum, fcum_t, grp.batch, grp.seq)
                else:
                    pool, page = cache_k.shape[1], cache_k.shape[2]
                    o = _attn_sample(q, k, v, logf,
                                     cache_k[i].reshape(pool, page, d_att), cache_v[i].reshape(pool, page, d_att),
                                     jnp.swapaxes(cache_logf[i], 1, 2), page_table)
                xs[gi] = _proj_residual(grp, o, w_o, x, mod[:, 2], "att_out")
                outs["k"][gi].append(k.reshape(grp.batch, grp.seq, H_ATT, HEAD_DIM))
                outs["v"][gi].append(v.reshape(grp.batch, grp.seq, H_ATT, HEAD_DIM))
                outs["lf"][gi].append(logf[:, :H_FOX].reshape(grp.batch, grp.seq, H_FOX))
        else:
            r_decay, r_aaa = rw_w1.shape[2], rw_a1.shape[2]
            w_rkv = jnp.concatenate([rw_w_r[i], rw_w_k[i], rw_w_v[i]], axis=1).astype(BF16)
            w_lr1 = jnp.concatenate([_pad_cols(rw_w1[i], LANES), _pad_cols(rw_a1[i], LANES), rw_g1[i]],
                                    axis=1).astype(BF16)
            n1 = w_lr1.shape[1]
            w_lr2 = jnp.zeros((n1, 3 * d), F32)
            w_lr2 = w_lr2.at[:r_decay, :d].set(rw_w2[i])
            w_lr2 = w_lr2.at[LANES:LANES + r_aaa, d:2 * d].set(rw_a2[i])
            w_lr2 = w_lr2.at[2 * LANES:, 2 * d:].set(rw_g2[i]).astype(BF16)
            bias_lr2 = jnp.concatenate([rw_w0[i], rw_a0[i], jnp.zeros((d,), F32)])[None, :]
            params = _pad_rows(jnp.stack([rw_k_k[i], rw_k_a[i], rw_r_k[i].reshape(d), rw_ln_w[i], rw_ln_b[i]]),
                               SUBLANES)
            w_o = rw_w_o[i].astype(BF16)
            for gi, grp in enumerate(groups):
                x, mod = xs[gi], mods[gi]
                h = _normmod(grp, x, norm_mix[l], mod[:, 0], mod[:, 1])
                h3 = h.reshape(grp.batch, grp.seq, d)
                if grp.seq > 1:
                    prev = jnp.concatenate([jnp.zeros((grp.batch, 1, d), F32), h3[:, :-1]], axis=1)
                    state0 = jnp.zeros((grp.batch, d // RWKV_HS, RWKV_HS, RWKV_HS), F32)
                    seq_pad, chunk = grp.seq, min(64, grp.seq)
                else:
                    prev = state_shift[i][:, None, :]
                    state0 = state_wkv[i]
                    seq_pad, chunk = SUBLANES, SUBLANES
                rkv, lag = _rwkv_project(grp, h, prev.reshape(grp.m, d), rw_mix[i], w_rkv, w_lr1, w_lr2, bias_lr2)
                pad = lambda a: jnp.pad(a.reshape(grp.batch, grp.seq, 3 * d), ((0, 0), (0, seq_pad - grp.seq), (0, 0)))
                z, s_fin = _rwkv_scan(pad(rkv), pad(lag), params, state0, grp.batch, seq_pad, chunk)
                z = z[:, :grp.seq].reshape(grp.m, d)
                xs[gi] = _proj_residual(grp, z, w_o, x, mod[:, 2], "rwkv_out")
                outs["wkv"][gi].append(s_fin)
                outs["shift"][gi].append(h3[:, -1])
        uv = jnp.concatenate([peer_u[l], peer_v[l]], axis=1)
        w_q = peer_wq[l].astype(BF16)
        for gi, grp in enumerate(groups):
            xs[gi] = _peer(grp, xs[gi], mods[gi], norm_ffn[l], w_q, peer_k1[l], peer_k2[l], uv)

    st = lambda name, gi: jnp.stack(outs[name][gi])
    return (xs[0].reshape(bp, tp, d), xs[1].reshape(bs, 1, d),
            st("k", 0), st("v", 0), st("lf", 0), st("wkv", 0), st("shift", 0),
            st("k", 1), st("v", 1), st("lf", 1), st("wkv", 1), st("shift", 1))
```

```python
import functools

import jax
import jax.numpy as jnp
from jax import lax
from jax.experimental import pallas as pl
from jax.experimental.pallas import tpu as pltpu

F32 = jnp.float32
BF16 = jnp.bfloat16

HEAD_DIM = 128
H_FOX = 8
H_ATT = 16
ATT_SCALE = HEAD_DIM ** -0.5
NEG_INF = -1e30
RWKV_HS = 64
LNX_EPS = 64e-5
PEER_HEADS = 8
PEER_TOPK = 16
N_KEYS = 128
N_MOD = 6
RMS_EPS = 1e-6

LANES = 128
SUBLANES = 8
VMEM_LIMIT_BYTES = 56 * 1024 * 1024


def _cparams(sem):
    return pltpu.CompilerParams(dimension_semantics=sem, vmem_limit_bytes=VMEM_LIMIT_BYTES)


def _split_bf16(x, parts):
    out = []
    for _ in range(parts - 1):
        hi = x.astype(BF16)
        out.append(hi)
        x = x - hi.astype(F32)
    out.append(x.astype(BF16))
    return out


def _dot_exact_rhs(x, rhs_bf16, parts=3):
    acc = None
    for p in _split_bf16(x, parts):
        t = jnp.dot(p, rhs_bf16, preferred_element_type=F32)
        acc = t if acc is None else acc + t
    return acc


def _dot_exact_lhs(lhs_bf16, x, parts=3):
    acc = None
    for p in _split_bf16(x, parts):
        t = jnp.dot(lhs_bf16, p, preferred_element_type=F32)
        acc = t if acc is None else acc + t
    return acc


def _dotf(a, b):
    a3 = _split_bf16(a, 3)
    b3 = _split_bf16(b, 3)
    acc = None
    for i, j in ((2, 0), (0, 2), (1, 1), (1, 0), (0, 1), (0, 0)):
        t = jnp.dot(a3[i], b3[j], preferred_element_type=F32)
        acc = t if acc is None else acc + t
    return acc


def _dotf_nt(a, b):
    a3 = _split_bf16(a, 3)
    b3 = _split_bf16(b, 3)
    acc = None
    dn = (((1,), (1,)), ((), ()))
    for i, j in ((2, 0), (0, 2), (1, 1), (1, 0), (0, 1), (0, 0)):
        t = lax.dot_general(a3[i], b3[j], dn, preferred_element_type=F32)
        acc = t if acc is None else acc + t
    return acc


def _softplus(z):
    return jnp.maximum(z, 0.0) + jnp.log1p(jnp.exp(-jnp.abs(z)))


def _sigmoid(z):
    return 1.0 / (1.0 + jnp.exp(-z))


class _Group:
    def __init__(self, batch, seq, tm):
        self.batch, self.seq = batch, seq
        self.m = batch * seq
        if seq == 1:
            self.tm = min(tm, self.m)
        else:
            self.tm = min(tm, seq)
        assert self.m % self.tm == 0 and (seq == 1 or seq % self.tm == 0)
        self.tiles_per_seq = max(seq // self.tm, 1)

    def seq_vec(self, v):
        return v if self.seq == 1 else v[:, None, :]

    def seq_spec(self, width, col=None):
        col = col or (lambda *ids: 0)
        if self.seq == 1:
            return pl.BlockSpec((self.tm, width), lambda *ids: (ids[0], col(*ids)))
        tps = self.tiles_per_seq
        return pl.BlockSpec((None, 1, width), lambda *ids: (ids[0] // tps, 0, col(*ids)))


def _normmod_body(x_ref, g_ref, shift_ref, scale_ref, o_ref):
    x = x_ref[...]
    h = x * lax.rsqrt(jnp.mean(x * x, axis=-1, keepdims=True) + RMS_EPS) * g_ref[...]
    o_ref[...] = h * (1.0 + scale_ref[...]) + shift_ref[...]


def _normmod(grp, x, g, shift, scale):
    m, d = x.shape
    return pl.pallas_call(
        _normmod_body,
        grid=(m // grp.tm,),
        in_specs=[pl.BlockSpec((grp.tm, d), lambda i: (i, 0)),
                  pl.BlockSpec((1, d), lambda i: (0, 0)),
                  grp.seq_spec(d), grp.seq_spec(d)],
        out_specs=pl.BlockSpec((grp.tm, d), lambda i: (i, 0)),
        out_shape=jax.ShapeDtypeStruct((m, d), F32),
        compiler_params=_cparams(("parallel",)),
        name="normmod",
    )(x, g[None, :], grp.seq_vec(shift), grp.seq_vec(scale))


def _mm_body(*refs, n_a, n_e, n_out, prologue, epilogue, every_j):
    a_refs = refs[:n_a]
    w_ref = refs[n_a]
    e_refs = refs[n_a + 1:n_a + 1 + n_e]
    o_refs = refs[n_a + 1 + n_e:n_a + 1 + n_e + n_out]
    a_scr = refs[-1]
    j = pl.program_id(1)

    def fill():
        a_scr[...] = prologue(*[r[...] for r in a_refs]).astype(BF16)

    if every_j:
        fill()
    else:
        pl.when(j == 0)(fill)
    acc = jnp.dot(a_scr[...], w_ref[...].astype(BF16), preferred_element_type=F32)
    epilogue(acc, j, e_refs, o_refs)


def _mm(m, tm, w, tn, a_ins, a_specs, e_ins, e_specs, out_shapes, out_specs, prologue, epilogue, name,
        every_j=False):
    k, n = w.shape
    assert m % tm == 0 and n % tn == 0
    body = functools.partial(_mm_body, n_a=len(a_ins), n_e=len(e_ins), n_out=len(out_shapes),
                             prologue=prologue, epilogue=epilogue, every_j=every_j)
    return pl.pallas_call(
        body,
        grid=(m // tm, n // tn),
        in_specs=list(a_specs) + [pl.BlockSpec((k, tn), lambda i, j: (0, j))] + list(e_specs),
        out_specs=list(out_specs),
        out_shape=list(out_shapes),
        scratch_shapes=[pltpu.VMEM((tm, k), BF16)],
        compiler_params=_cparams(("parallel", "arbitrary")),
        name=name,
    )(*a_ins, w, *e_ins)


def _tile_spec(tm, width):
    return pl.BlockSpec((tm, width), lambda i, j: (i, 0))


def _out_spec(tm, tn):
    return pl.BlockSpec((tm, tn), lambda i, j: (i, j))


def _row_spec(tn):
    return pl.BlockSpec((1, tn), lambda i, j: (0, j))


def _ident(a):
    return a


def _epi(fn):
    def run(acc, j, e_refs, o_refs):
        o_refs[0][...] = fn(acc, j, *[r[...] for r in e_refs])
    return run


def _ada_mod(c, w, b):
    m, d = c.shape
    n = w.shape[1]
    tn = 512
    (out,) = _mm(m, m, w, tn, [c], [_tile_spec(m, d)], [b[None, :]], [_row_spec(tn)],
                 [jax.ShapeDtypeStruct((m, n), F32)], [_out_spec(m, tn)],
                 lambda cc: cc * _sigmoid(cc), _epi(lambda acc, j, bb: acc + bb), "ada_mod")
    return out


def _head_rmsnorm(y, gain):
    blocks = []
    for hh in range(y.shape[1] // HEAD_DIM):
        blk = y[:, hh * HEAD_DIM:(hh + 1) * HEAD_DIM]
        ms = jnp.mean(blk * blk, axis=-1, keepdims=True)
        blocks.append(blk * lax.rsqrt(ms + RMS_EPS) * gain)
    return jnp.concatenate(blocks, axis=1)


def _qkv_epilogue(acc, j, e_refs, o_refs):
    gains = e_refs[0]
    q_ref, k_ref, v_ref = o_refs

    @pl.when(j == 0)
    def _():
        q_ref[...] = _head_rmsnorm(acc, gains[0:1, :])

    @pl.when(j == 1)
    def _():
        q_ref[...] = acc

    @pl.when(j == 2)
    def _():
        k_ref[...] = _head_rmsnorm(acc, gains[1:2, :])

    @pl.when(j == 3)
    def _():
        k_ref[...] = acc

    @pl.when(j >= 4)
    def _():
        v_ref[...] = acc


def _att_project(grp, h, w_qkv, w_f, q_gain, k_gain, f_bias):
    m, d = h.shape
    tm = grp.tm
    d_att = H_ATT * HEAD_DIM
    tn = d_att // 2
    gains = jnp.stack([q_gain, k_gain])
    shp = jax.ShapeDtypeStruct((m, d_att), F32)

    def ospec(first):
        return pl.BlockSpec((tm, tn), lambda i, j: (i, jnp.clip(j - first, 0, 1)))

    q, k, v = _mm(m, tm, w_qkv, tn, [h], [_tile_spec(tm, d)],
                  [gains], [pl.BlockSpec((2, HEAD_DIM), lambda i, j: (0, 0))],
                  [shp, shp, shp], [ospec(0), ospec(2), ospec(4)],
                  _ident, _qkv_epilogue, "att_qkv")
    (logf,) = _mm(m, tm, w_f, LANES, [h], [_tile_spec(tm, d)],
                  [f_bias], [_row_spec(LANES)],
                  [jax.ShapeDtypeStruct((m, LANES), F32)], [_out_spec(tm, LANES)],
                  _ident, _epi(lambda acc, j, fb: -_softplus(-(acc + fb))), "att_logf")
    return q, k, v, logf


def _cumsum_body(x_ref, o_ref):
    t = x_ref.shape[0]
    row = lax.broadcasted_iota(jnp.int32, (t, t), 0)
    col = lax.broadcasted_iota(jnp.int32, (t, t), 1)
    tri = (col <= row).astype(BF16)
    o_ref[...] = _dot_exact_lhs(tri, x_ref[...])


def _cumsum_seq(x):
    b, t, w = x.shape
    return pl.pallas_call(
        _cumsum_body,
        grid=(b,),
        in_specs=[pl.BlockSpec((None, t, w), lambda i: (i, 0, 0))],
        out_specs=pl.BlockSpec((None, t, w), lambda i: (i, 0, 0)),
        out_shape=jax.ShapeDtypeStruct((b, t, w), F32),
        compiler_params=_cparams(("parallel",)),
        name="fox_cumsum",
    )(x)


def _attn_prompt_body(q_ref, k_ref, v_ref, fc_ref, ft_ref, o_ref, *, tq):
    h = pl.program_id(1)
    qi = pl.program_id(2)
    q = q_ref[...].astype(BF16)
    t_pos = qi * tq + lax.broadcasted_iota(jnp.int32, (tq, tq), 0)
    s_off = lax.broadcasted_iota(jnp.int32, (tq, tq), 1)
    nt = (((1,), (1,)), ((), ()))

    def scores(kb):
        start = pl.multiple_of(kb * tq, tq)
        kblk = k_ref[pl.ds(start, tq), :].astype(BF16)
        vblk = v_ref[pl.ds(start, tq), :].astype(BF16)
        s = lax.dot_general(q, kblk, nt, preferred_element_type=F32) * ATT_SCALE
        return s, vblk, start

    @pl.when(h < H_FOX)
    def _fox():
        lane = lax.broadcasted_iota(jnp.int32, fc_ref.shape, 1)
        fq = jnp.sum(jnp.where(lane == h, fc_ref[...], 0.0), axis=-1, keepdims=True)

        def step(kb, carry):
            m_run, l_run, acc = carry
            s, vblk, start = scores(kb)
            fk = ft_ref[pl.ds(h, 1), pl.ds(start, tq)]
            s = s + (fq - fk)
            s = jnp.where(kb * tq + s_off <= t_pos, s, NEG_INF)
            m_new = jnp.maximum(m_run, jnp.max(s, axis=-1, keepdims=True))
            alpha = jnp.exp(m_run - m_new)
            p = jnp.exp(s - m_new)
            l_new = alpha * l_run + jnp.sum(p, axis=-1, keepdims=True)
            acc = alpha * acc + jnp.dot(p.astype(BF16), vblk, preferred_element_type=F32)
            return m_new, l_new, acc

        init = (jnp.full((tq, 1), NEG_INF, F32), jnp.zeros((tq, 1), F32), jnp.zeros((tq, HEAD_DIM), F32))
        _, l_fin, acc = lax.fori_loop(0, qi + 1, step, init)
        o_ref[...] = acc / l_fin

    @pl.when(h >= H_FOX)
    def _sb():
        jj = lax.broadcasted_iota(jnp.int32, (tq, tq), 0)
        ss = lax.broadcasted_iota(jnp.int32, (tq, tq), 1)
        upper = (jj > ss).astype(BF16)

        def step(it, carry):
            later, acc = carry
            kb = qi - it
            z, vblk, _ = scores(kb)
            strict = kb * tq + s_off < t_pos
            lk = jnp.where(strict, -_softplus(z), 0.0)
            later_in = _dot_exact_rhs(lk, upper, parts=2)
            a = jnp.where(strict, jnp.exp(z + lk + later_in + later), 0.0)
            acc = acc + jnp.dot(a.astype(BF16), vblk, preferred_element_type=F32)
            later = later + jnp.sum(lk, axis=-1, keepdims=True)
            return later, acc

        init = (jnp.zeros((tq, 1), F32), jnp.zeros((tq, HEAD_DIM), F32))
        _, acc = lax.fori_loop(0, qi + 1, step, init)
        o_ref[...] = acc


def _attn_prompt(q, k, v, fcum, fcum_t, batch, seq):
    d_att = q.shape[1]
    tq = min(256, seq)
    q3, k3, v3 = (a.reshape(batch, seq, d_att) for a in (q, k, v))
    body = functools.partial(_attn_prompt_body, tq=tq)
    out = pl.pallas_call(
        body,
        grid=(batch, H_ATT, seq // tq),
        in_specs=[pl.BlockSpec((None, tq, HEAD_DIM), lambda b, h, i: (b, i, h)),
                  pl.BlockSpec((None, seq, HEAD_DIM), lambda b, h, i: (b, 0, h)),
                  pl.BlockSpec((None, seq, HEAD_DIM), lambda b, h, i: (b, 0, h)),
                  pl.BlockSpec((None, tq, LANES), lambda b, h, i: (b, i, 0)),
                  pl.BlockSpec((None, H_FOX, seq), lambda b, h, i: (b, 0, 0))],
        out_specs=pl.BlockSpec((None, tq, HEAD_DIM), lambda b, h, i: (b, i, h)),
        out_shape=jax.ShapeDtypeStruct((batch, seq, d_att), F32),
        compiler_params=_cparams(("parallel", "parallel", "arbitrary")),
        name="attn_prompt",
    )(q3, k3, v3, fcum, fcum_t)
    return out.reshape(batch * seq, d_att)


def _attn_sample_body(pt_ref, q_ref, kn_ref, vn_ref, lfn_ref, kc_ref, vc_ref, lfc_ref, o_ref,
                      qrows, m_scr, l_scr, suf_scr, acc_scr, *, n_pages):
    p = pl.program_id(1)
    d_att = q_ref.shape[-1]
    page = kc_ref.shape[0]
    sub = lax.broadcasted_iota(jnp.int32, (H_ATT, d_att), 0)
    lane_head = lax.broadcasted_iota(jnp.int32, (H_ATT, d_att), 1) // HEAD_DIM
    own = sub == lane_head
    is_fox = lax.broadcasted_iota(jnp.int32, (H_ATT, 1), 0) < H_FOX
    nt = (((1,), (1,)), ((), ()))

    @pl.when(p == 0)
    def _init():
        qr = jnp.where(own, q_ref[...], 0.0)
        qrows[...] = qr.astype(BF16)
        s_self = jnp.sum(qr * kn_ref[...], axis=-1, keepdims=True) * ATT_SCALE
        m_scr[...] = jnp.where(is_fox, s_self, 0.0)
        l_scr[...] = jnp.ones_like(l_scr)
        acc_scr[...] = jnp.where(sub < H_FOX, jnp.broadcast_to(vn_ref[...], (H_ATT, d_att)), 0.0)
        lf_lane = lax.broadcasted_iota(jnp.int32, (H_ATT, LANES), 1)
        lf_sub = lax.broadcasted_iota(jnp.int32, (H_ATT, LANES), 0)
        lf_new = jnp.sum(jnp.where(lf_lane == lf_sub, lfn_ref[...], 0.0), axis=-1, keepdims=True)
        suf_scr[...] = jnp.where(is_fox, lf_new, 0.0)

    s = lax.dot_general(qrows[...], kc_ref[...].astype(BF16), nt,
                        preferred_element_type=F32) * ATT_SCALE
    jj = lax.broadcasted_iota(jnp.int32, (page, page), 0)
    ss = lax.broadcasted_iota(jnp.int32, (page, page), 1)
    upper = (jj > ss).astype(BF16)
    lk = -_softplus(s)
    lf = jnp.concatenate([lfc_ref[...], jnp.zeros((H_ATT - H_FOX, page), F32)], axis=0)
    step_log = jnp.where(is_fox, lf, lk)
    later = _dot_exact_rhs(step_log, upper) + suf_scr[...]
    logit = s + later + jnp.where(is_fox, 0.0, lk)
    m_old = m_scr[...]
    m_new = jnp.where(is_fox, jnp.maximum(m_old, jnp.max(logit, axis=-1, keepdims=True)), 0.0)
    alpha = jnp.exp(m_old - m_new)
    w = jnp.exp(logit - m_new)
    l_scr[...] = jnp.where(is_fox, alpha * l_scr[...] + jnp.sum(w, axis=-1, keepdims=True), 1.0)
    acc_scr[...] = alpha * acc_scr[...] + jnp.dot(w.astype(BF16), vc_ref[...].astype(BF16),
                                                  preferred_element_type=F32)
    m_scr[...] = m_new
    suf_scr[...] = suf_scr[...] + jnp.sum(step_log, axis=-1, keepdims=True)

    @pl.when(p == n_pages - 1)
    def _fin():
        full = acc_scr[...] / l_scr[...]
        o_ref[...] = jnp.sum(jnp.where(own, full, 0.0), axis=0, keepdims=True)


def _attn_sample(q, k_new, v_new, logf_new, cache_k, cache_v, cache_lf_t, page_table):
    b, d_att = q.shape
    n_pages = page_table.shape[1]
    page = cache_k.shape[1]
    row = lambda a: a[:, None, :]

    def new_spec(width):
        return pl.BlockSpec((None, 1, width), lambda i, p, pt: (i, 0, 0))

    def page_spec(shape):
        return pl.BlockSpec((None,) + shape, lambda i, p, pt: (pt[i, n_pages - 1 - p], 0, 0))

    grid_spec = pltpu.PrefetchScalarGridSpec(
        num_scalar_prefetch=1,
        grid=(b, n_pages),
        in_specs=[new_spec(d_att), new_spec(d_att), new_spec(d_att), new_spec(LANES),
                  page_spec((page, d_att)), page_spec((page, d_att)), page_spec((H_FOX, page))],
        out_specs=pl.BlockSpec((None, 1, d_att), lambda i, p, pt: (i, 0, 0)),
        scratch_shapes=[pltpu.VMEM((H_ATT, d_att), BF16), pltpu.VMEM((H_ATT, 1), F32),
                        pltpu.VMEM((H_ATT, 1), F32), pltpu.VMEM((H_ATT, 1), F32),
                        pltpu.VMEM((H_ATT, d_att), F32)],
    )
    out = pl.pallas_call(
        functools.partial(_attn_sample_body, n_pages=n_pages),
        grid_spec=grid_spec,
        out_shape=jax.ShapeDtypeStruct((b, 1, d_att), F32),
        compiler_params=_cparams(("parallel", "arbitrary")),
        name="attn_sample",
    )(page_table, row(q), row(k_new), row(v_new), row(logf_new), cache_k, cache_v, cache_lf_t)
    return out.reshape(b, d_att)


def _proj_residual(grp, a, w, x, gate, name):
    m, d = x.shape
    tm, tn = grp.tm, min(1024, d)
    (out,) = _mm(m, tm, w, tn, [a], [_tile_spec(tm, a.shape[1])],
                 [x, grp.seq_vec(gate)], [_out_spec(tm, tn), grp.seq_spec(tn, col=lambda i, j: j)],
                 [jax.ShapeDtypeStruct((m, d), F32)], [_out_spec(tm, tn)],
                 _ident, _epi(lambda acc, j, xx, gg: xx + gg * acc), name)
    return out


def _mix_prologue(h, prev, mix):
    return h + (prev - h) * mix


def _rwkv_project(grp, h, prev, mix, w_rkv, w_lr1, w_lr2, bias_lr2):
    m, d = h.shape
    tm = grp.tm
    tn = d // 2
    mix_rkv = jnp.stack([mix[0], mix[2], mix[3]])[:, None, :]
    (rkv,) = _mm(m, tm, w_rkv, tn, [h, prev, mix_rkv],
                 [_tile_spec(tm, d), _tile_spec(tm, d),
                  pl.BlockSpec((None, 1, d), lambda i, j: (j // 2, 0, 0))],
                 [], [], [jax.ShapeDtypeStruct((m, 3 * d), F32)], [_out_spec(tm, tn)],
                 _mix_prologue, _epi(lambda acc, j: acc), "rwkv_rkv", every_j=True)

    mix_lr = jnp.stack([mix[1], mix[4], mix[5], mix[5]])[:, None, :]

    def lr1_act(acc, j):
        return lax.cond(j == 0, jnp.tanh, lambda a: lax.cond(j == 1, _ident, _sigmoid, a), acc)

    n1 = w_lr1.shape[1]
    (lr1,) = _mm(m, tm, w_lr1, LANES, [h, prev, mix_lr],
                 [_tile_spec(tm, d), _tile_spec(tm, d),
                  pl.BlockSpec((None, 1, d), lambda i, j: (j, 0, 0))],
                 [], [], [jax.ShapeDtypeStruct((m, n1), F32)], [_out_spec(tm, LANES)],
                 _mix_prologue, _epi(lr1_act), "rwkv_lr1", every_j=True)

    def lr2_act(acc, j, bias):
        y = acc + bias
        decay = lambda a: -jnp.exp(-_softplus(-a) - 0.5)
        return lax.cond(j < 2, decay, lambda a: lax.cond(j < 4, _sigmoid, _ident, a), y)

    (lag,) = _mm(m, tm, w_lr2, tn, [lr1], [_tile_spec(tm, n1)],
                 [bias_lr2], [_row_spec(tn)],
                 [jax.ShapeDtypeStruct((m, 3 * d), F32)], [_out_spec(tm, tn)],
                 _ident, _epi(lr2_act), "rwkv_lr2")
    return rkv, lag


def _dot3(a, b, dn):
    return lax.dot_general(a.astype(BF16), b.astype(BF16), dn, preferred_element_type=F32)


_NN = (((1,), (0,)), ((), ()))
_NT = (((1,), (1,)), ((), ()))


def _half_sum(x, lo_half):
    s0 = jnp.sum(jnp.where(lo_half, x, 0.0), axis=-1, keepdims=True)
    s1 = jnp.sum(jnp.where(lo_half, 0.0, x), axis=-1, keepdims=True)
    return jnp.where(lo_half, s0, s1)


def _rwkv_scan_body(r_ref, k_ref, v_ref, lw_ref, a_ref, g_ref, par_ref, s0_ref, z_ref, sT_ref,
                    s_scr, a2_scr, u0_scr, o0_scr, rh_scr, bp_scr, no_scr, kv_scr, last_scr,
                    *, chunk, n_chunks, n_pairs):
    hs = RWKV_HS
    blk = 2 * hs
    lane = lax.broadcasted_iota(jnp.int32, (1, blk), 1)
    lo_half = lane < hs
    hi_half = jnp.logical_not(lo_half)
    row = lax.broadcasted_iota(jnp.int32, (chunk, chunk), 0)
    col = lax.broadcasted_iota(jnp.int32, (chunk, chunk), 1)
    tril_incl = (col <= row)
    tril_strict = (col < row)
    eye = (col == row).astype(F32)
    rr = lax.broadcasted_iota(jnp.int32, (blk, blk), 0) < hs
    cc = lax.broadcasted_iota(jnp.int32, (blk, blk), 1) < hs
    same_head = rr == cc
    n_double = max(chunk.bit_length() - 2, 0)
    zero = jnp.zeros((hs, hs), F32)

    def lanes(pi):
        return slice(pi * blk, (pi + 1) * blk)

    def params(pi):
        return [par_ref[i:i + 1, lanes(pi)] for i in range(5)]

    def prepare(c, pi):
        sl = pl.ds(pl.multiple_of(c * chunk, chunk), chunk)
        r, k, v, lw, a = (ref[sl, lanes(pi)] for ref in (r_ref, k_ref, v_ref, lw_ref, a_ref))
        k_k, k_a = params(pi)[:2]
        kk = k * k_k
        kk = kk / jnp.maximum(jnp.sqrt(_half_sum(kk * kk, lo_half)), 1e-12)
        kmod = k * (1.0 + (a - 1.0) * k_a)
        cum = _dot_exact_lhs(tril_incl.astype(BF16), lw)
        dec_in = jnp.exp(cum)
        inv = jnp.exp(-cum)
        a_hat, b_hat, k_hat, r_hat = -kk * jnp.exp(cum - lw), kk * a * inv, kmod * inv, r * dec_in
        last = dec_in[chunk - 1:chunk, :]

        def solve(msk):
            am = jnp.where(msk, a_hat, 0.0)
            n_mat = jnp.where(tril_strict, _dot3(am, b_hat, _NT), 0.0)
            m_mat = jnp.where(tril_strict, _dot3(am, k_hat, _NT), 0.0)
            inv_mat, pw = eye + n_mat, n_mat
            for _ in range(n_double):
                pw = _dot3(pw, pw, _NN)
                inv_mat = inv_mat + _dot3(inv_mat, pw, _NN)
            return _dot3(inv_mat, a_hat, _NN), _dot3(inv_mat, _dot3(m_mat, v, _NN), _NN)

        a2_lo, u0_lo = solve(lo_half)
        a2_hi, u0_hi = solve(hi_half)
        a2_scr[pi, c] = jnp.where(lo_half, a2_lo, a2_hi)
        u0_scr[pi, c] = jnp.where(lo_half, u0_lo, u0_hi)

        def readout(msk):
            rm = jnp.where(msk, r_hat, 0.0)
            no = jnp.where(tril_incl, _dot3(rm, b_hat, _NT), 0.0)
            mo = jnp.where(tril_incl, _dot3(rm, k_hat, _NT), 0.0)
            return no, _dot3(mo, v, _NN)

        no_lo, o0_lo = readout(lo_half)
        no_hi, o0_hi = readout(hi_half)
        no_scr[pi, c, 0] = no_lo
        no_scr[pi, c, 1] = no_hi
        o0_scr[pi, c] = jnp.where(lo_half, o0_lo, o0_hi)
        rh_scr[pi, c] = r_hat
        bp_scr[pi, c] = b_hat * last
        kv_scr[pi, c] = jnp.where(same_head, _dot3(v.T, k_hat * last, _NN), 0.0)
        last_scr[pi, c] = jnp.broadcast_to(last, (SUBLANES, blk))

    def phase1(c, _):
        for pi in range(n_pairs):
            prepare(c, pi)
        return 0

    lax.fori_loop(0, n_chunks, phase1, 0, unroll=2 if n_chunks % 2 == 0 else 1)

    for pi in range(n_pairs):
        s_scr[pi] = jnp.concatenate([jnp.concatenate([s0_ref[2 * pi], zero], axis=1),
                                     jnp.concatenate([zero, s0_ref[2 * pi + 1]], axis=1)], axis=0)

    def advance(c, pi):
        sl = pl.ds(pl.multiple_of(c * chunk, chunk), chunk)
        state = s_scr[pi]
        last = last_scr[pi, c][0:1, :]
        u = u0_scr[pi, c] + _dot3(a2_scr[pi, c], state, _NT)
        s_scr[pi] = jnp.where(same_head, state * last + _dot3(u.T, bp_scr[pi, c], _NN), 0.0) + kv_scr[pi, c]
        o = (_dot3(rh_scr[pi, c], state, _NT) + o0_scr[pi, c]
             + jnp.where(lo_half, _dot3(no_scr[pi, c, 0], u, _NN), _dot3(no_scr[pi, c, 1], u, _NN)))

        r, k, v, a, g = (ref[sl, lanes(pi)] for ref in (r_ref, k_ref, v_ref, a_ref, g_ref))
        _, k_a, r_k, ln_w, ln_b = params(pi)
        kmod = k * (1.0 + (a - 1.0) * k_a)
        mu = _half_sum(o, lo_half) / hs
        dev = o - mu
        var = _half_sum(dev * dev, lo_half) / hs
        o_n = dev * lax.rsqrt(var + LNX_EPS) * ln_w + ln_b
        bonus = _half_sum(r * kmod * r_k, lo_half) * v
        z_ref[sl, lanes(pi)] = (o_n + bonus) * g

    def phase2(c, _):
        for pi in range(n_pairs):
            advance(c, pi)
        return 0

    lax.fori_loop(0, n_chunks, phase2, 0)
    for pi in range(n_pairs):
        fin = s_scr[pi]
        sT_ref[2 * pi] = fin[:hs, :hs]
        sT_ref[2 * pi + 1] = fin[hs:, hs:]


def _rwkv_scan(rkv, lag, params, state0, batch, seq, chunk, n_pairs):
    d = rkv.shape[-1] // 3
    blk = 2 * RWKV_HS
    width = n_pairs * blk
    groups = d // width
    n_chunks = seq // chunk

    def seq_spec(part):
        return pl.BlockSpec((None, seq, width), lambda b, p: (b, 0, part * groups + p))

    st_spec = pl.BlockSpec((None, 2 * n_pairs, RWKV_HS, RWKV_HS), lambda b, p: (b, p, 0, 0))
    body = functools.partial(_rwkv_scan_body, chunk=chunk, n_chunks=n_chunks, n_pairs=n_pairs)
    per_chunk = lambda *shape: pltpu.VMEM((n_pairs, n_chunks) + shape, F32)
    z, s_fin = pl.pallas_call(
        body,
        grid=(batch, groups),
        in_specs=[seq_spec(0), seq_spec(1), seq_spec(2), seq_spec(0), seq_spec(1), seq_spec(2),
                  pl.BlockSpec((SUBLANES, width), lambda b, p: (0, p)), st_spec],
        out_specs=[pl.BlockSpec((None, seq, width), lambda b, p: (b, 0, p)), st_spec],
        out_shape=[jax.ShapeDtypeStruct((batch, seq, d), F32),
                   jax.ShapeDtypeStruct(state0.shape, F32)],
        scratch_shapes=[pltpu.VMEM((n_pairs, blk, blk), F32),
                        per_chunk(chunk, blk), per_chunk(chunk, blk), per_chunk(chunk, blk),
                        per_chunk(chunk, blk), per_chunk(chunk, blk), per_chunk(2, chunk, chunk),
                        per_chunk(blk, blk), per_chunk(SUBLANES, blk)],
        compiler_params=_cparams(("parallel", "parallel")),
        name="rwkv_scan",
    )(rkv, rkv, rkv, lag, lag, lag, params, state0)
    return z, s_fin


def _rwkv_step_body(rkv_ref, lag_ref, par_ref, s0_ref, z_ref, s1_ref):
    hs = RWKV_HS
    r, k, v = rkv_ref[0], rkv_ref[1], rkv_ref[2]
    lw, a, g = lag_ref[0], lag_ref[1], lag_ref[2]
    k_k, k_a, r_k, ln_w, ln_b = (par_ref[i] for i in range(5))
    eye = (lax.broadcasted_iota(jnp.int32, (hs, hs), 0) == lax.broadcasted_iota(jnp.int32, (hs, hs), 1))
    kk = k * k_k
    kk = kk / jnp.maximum(jnp.sqrt(jnp.sum(kk * kk, axis=-1, keepdims=True)), 1e-12)
    kmod = k * (1.0 + (a - 1.0) * k_a)
    state = s0_ref[...]
    sa = jnp.sum(state * (-kk), axis=-1, keepdims=True)
    v_col = jnp.sum(jnp.where(eye, v, 0.0), axis=-1, keepdims=True)
    new = state * jnp.exp(lw) + sa * (kk * a) + v_col * kmod
    s1_ref[...] = new
    o_col = jnp.sum(new * r, axis=-1, keepdims=True)
    o = jnp.sum(jnp.where(eye, o_col, 0.0), axis=1, keepdims=True)
    mu = jnp.mean(o, axis=-1, keepdims=True)
    dev = o - mu
    var = jnp.mean(dev * dev, axis=-1, keepdims=True)
    o_n = dev * lax.rsqrt(var + LNX_EPS) * ln_w + ln_b
    bonus = jnp.sum(r * kmod * r_k, axis=-1, keepdims=True) * v
    z_ref[...] = (o_n + bonus) * g


def _rwkv_step(rkv, lag, params, state0):
    b, heads, hs, _ = state0.shape
    vec = lambda x: x.reshape(b, 3, heads, 1, hs)
    vec_spec = pl.BlockSpec((None, 3, heads, 1, hs), lambda i: (i, 0, 0, 0, 0))
    st_spec = pl.BlockSpec((None, heads, hs, hs), lambda i: (i, 0, 0, 0))
    z, s_fin = pl.pallas_call(
        _rwkv_step_body,
        grid=(b,),
        in_specs=[vec_spec, vec_spec, pl.BlockSpec((SUBLANES, heads, 1, hs), lambda i: (0, 0, 0, 0)), st_spec],
        out_specs=[pl.BlockSpec((None, heads, 1, hs), lambda i: (i, 0, 0, 0)), st_spec],
        out_shape=[jax.ShapeDtypeStruct((b, heads, 1, hs), F32), jax.ShapeDtypeStruct(state0.shape, F32)],
        compiler_params=_cparams(("parallel",)),
        name="rwkv_step",
    )(vec(rkv), vec(lag), params.reshape(SUBLANES, heads, 1, hs), state0)
    return z.reshape(b, heads * hs), s_fin


def _peer_route_body(q_ref, k1_ref, k2_ref, ids_ref, gate_ref):
    tm = q_ref.shape[0]
    topk = PEER_TOPK
    n_cand = topk * topk
    half = k1_ref.shape[-1]
    key_row = lax.broadcasted_iota(jnp.int32, (N_KEYS, tm), 0).astype(F32)
    cand_row = lax.broadcasted_iota(jnp.int32, (n_cand, tm), 0).astype(F32)
    rank = lax.broadcasted_iota(jnp.int32, (topk, tm), 0)
    rank_f = rank.astype(F32)

    def take_max(s, rows, n_rows):
        m = jnp.max(s, axis=0, keepdims=True)
        idx = jnp.min(jnp.where(s == m, rows, float(n_rows)), axis=0, keepdims=True)
        return m, idx, jnp.where(rows == idx, -jnp.inf, s)

    vals_heads, ids_heads = [], []
    for hq in range(PEER_HEADS):
        q1 = q_ref[:, hq * 2 * half:hq * 2 * half + half].astype(BF16)
        q2 = q_ref[:, hq * 2 * half + half:(hq + 1) * 2 * half].astype(BF16)
        s1 = lax.dot_general(k1_ref[hq].astype(BF16), q1, _NT, preferred_element_type=F32)
        s2 = lax.dot_general(k2_ref[hq].astype(BF16), q2, _NT, preferred_element_type=F32)

        def sub_it(i, carry):
            s1, s2, v1, e1, v2, e2 = carry
            m1, i1, s1 = take_max(s1, key_row, N_KEYS)
            m2, i2, s2 = take_max(s2, key_row, N_KEYS)
            here = rank == i
            return (s1, s2, jnp.where(here, m1, v1), jnp.where(here, i1, e1),
                    jnp.where(here, m2, v2), jnp.where(here, i2, e2))

        zero = jnp.zeros((topk, tm), F32)
        _, _, v1, e1, v2, e2 = lax.fori_loop(0, topk, sub_it, (s1, s2, zero, zero, zero, zero))
        cs = jnp.concatenate([v1[i:i + 1, :] + v2 for i in range(topk)], axis=0)

        def cand_it(i, carry):
            cs, vals, ids = carry
            m, pos, cs = take_max(cs, cand_row, n_cand)
            hi = jnp.floor(pos * (1.0 / topk))
            lo = pos - hi * topk
            k_hi = jnp.sum(jnp.where(rank_f == hi, e1, 0.0), axis=0, keepdims=True)
            k_lo = jnp.sum(jnp.where(rank_f == lo, e2, 0.0), axis=0, keepdims=True)
            here = rank == i
            return cs, jnp.where(here, m, vals), jnp.where(here, k_hi * N_KEYS + k_lo, ids)

        _, vals, ids = lax.fori_loop(0, topk, cand_it, (cs, zero, zero))
        ex = jnp.exp(vals - jnp.max(vals, axis=0, keepdims=True))
        vals_heads.append(ex / jnp.sum(ex, axis=0, keepdims=True))
        ids_heads.append(ids)

    gate_ref[...] = jnp.concatenate(vals_heads, axis=0).T
    ids_ref[...] = jnp.concatenate(ids_heads, axis=0).T.astype(jnp.int32)


def _peer_route(q, k1, k2, tm):
    m, dq = q.shape
    assert PEER_HEADS * PEER_TOPK == N_KEYS
    kspec = pl.BlockSpec(k1.shape, lambda i: (0, 0, 0))
    ospec = pl.BlockSpec((tm, N_KEYS), lambda i: (i, 0))
    return pl.pallas_call(
        _peer_route_body,
        grid=(m // tm,),
        in_specs=[pl.BlockSpec((tm, dq), lambda i: (i, 0)), kspec, kspec],
        out_specs=[ospec, ospec],
        out_shape=[jax.ShapeDtypeStruct((m, N_KEYS), jnp.int32), jax.ShapeDtypeStruct((m, N_KEYS), F32)],
        compiler_params=_cparams(("parallel",)),
        name="peer_route",
    )(q, k1, k2)


def _gelu(x):
    return 0.5 * x * (1.0 + lax.erf(x * (2.0 ** -0.5)))


def _peer_expert_body(ids_ref, nxt_ref, h_ref, gate_ref, x_ref, g5_ref, uv_hbm, o_ref, buf0, buf1, sem, pout,
                      *, tt):
    d = h_ref.shape[1]
    n_sel = gate_ref.shape[1]
    step = pl.program_id(0)
    last_step = pl.num_programs(0) - 1
    bufs = (buf0, buf1)
    diag = (lax.broadcasted_iota(jnp.int32, (n_sel, n_sel), 0)
            == lax.broadcasted_iota(jnp.int32, (n_sel, n_sel), 1))

    def issue(src_ref, row, slot):
        for p in range(n_sel):
            pltpu.make_async_copy(uv_hbm.at[pl.ds(src_ref[row, p], 1)], bufs[slot].at[pl.ds(p, 1)],
                                  sem.at[slot]).start()

    def wait_all(slot):
        pltpu.make_async_copy(uv_hbm.at[pl.ds(0, n_sel)], bufs[slot], sem.at[slot]).wait()

    def compute(t, slot):
        act = jnp.sum(bufs[slot][:, :d] * h_ref[pl.ds(t, 1), :], axis=1, keepdims=True)
        gate = jnp.sum(jnp.where(diag, gate_ref[pl.ds(t, 1), :], 0.0), axis=1, keepdims=True)
        wgt = _gelu(act) * gate
        pout[pl.ds(t, 1), :] = jnp.sum(bufs[slot][:, d:] * wgt, axis=0, keepdims=True)

    @pl.when(step == 0)
    def _():
        issue(ids_ref, 0, 0)

    def token_pair(k, _):
        t0 = 2 * k
        wait_all(0)
        issue(ids_ref, t0 + 1, 1)
        compute(t0, 0)
        wait_all(1)
        issue(ids_ref, t0 + 2, 0)
        compute(t0 + 1, 1)
        return 0

    lax.fori_loop(0, tt // 2 - 1, token_pair, 0)
    wait_all(0)
    issue(ids_ref, tt - 1, 1)
    compute(tt - 2, 0)
    wait_all(1)
    issue(nxt_ref, 0, 0)
    compute(tt - 1, 1)

    @pl.when(step == last_step)
    def _():
        wait_all(0)

    o_ref[...] = x_ref[...] + g5_ref[...] * pout[...]


def _peer_experts(grp, h, ids, gates, x, gate5, uv, tt):
    m, d = h.shape
    n_sel = ids.shape[1]
    tt = min(tt, grp.tm)
    assert grp.tm % tt == 0 and tt % 2 == 0
    sub = grp.tm // tt
    n_steps = m // tt
    ids_spec = lambda nxt: pl.BlockSpec((tt, n_sel), lambda i: (jnp.minimum(i + nxt, n_steps - 1), 0),
                                        memory_space=pltpu.SMEM)
    tile = lambda width: pl.BlockSpec((tt, width), lambda i: (i, 0))
    if grp.seq == 1:
        g5, g5_spec = gate5, tile(d)
    else:
        tps = grp.tiles_per_seq * sub
        g5, g5_spec = gate5[:, None, :], pl.BlockSpec((None, 1, d), lambda i: (i // tps, 0, 0))
    return pl.pallas_call(
        functools.partial(_peer_expert_body, tt=tt),
        grid=(n_steps,),
        in_specs=[ids_spec(0), ids_spec(1),
                  tile(d), tile(n_sel), tile(d), g5_spec,
                  pl.BlockSpec(memory_space=pl.ANY)],
        out_specs=tile(d),
        out_shape=jax.ShapeDtypeStruct((m, d), F32),
        scratch_shapes=[pltpu.VMEM((n_sel, 2 * d), F32), pltpu.VMEM((n_sel, 2 * d), F32),
                        pltpu.SemaphoreType.DMA((2,)), pltpu.VMEM((tt, d), F32)],
        compiler_params=_cparams(("arbitrary",)),
        name="peer_experts",
    )(ids, ids, h, gates, x, g5, uv)


def _peer(grp, x, mod, norm_g, w_q, k1, k2, uv):
    h = _normmod(grp, x, norm_g, mod[:, 3], mod[:, 4])
    m, d = h.shape
    tn = min(1024, w_q.shape[1])
    (q,) = _mm(m, grp.tm, w_q, tn, [h], [_tile_spec(grp.tm, d)], [], [],
               [jax.ShapeDtypeStruct((m, w_q.shape[1]), F32)], [_out_spec(grp.tm, tn)],
               _ident, _epi(lambda acc, j: acc), "peer_q")
    ids, gates = _peer_route(q, k1, k2, min(128, m))
    return _peer_experts(grp, h, ids, gates, x, mod[:, 5], uv, 32)


def _pad_cols(w, width):
    return jnp.pad(w, ((0, 0), (0, width - w.shape[1])))


def _pad_rows(w, height):
    return jnp.pad(w, ((0, height - w.shape[0]), (0, 0)))


def kernel(x_prompt, x_sample, cache_k, cache_v, cache_logf, state_wkv, state_shift, page_table, c_prompt, c_sample, ada_w, ada_b, norm_mix, norm_ffn, att_w_in, att_w_o, att_q_gain, att_k_gain, att_f_bias, rw_mix, rw_w0, rw_w1, rw_w2, rw_a0, rw_a1, rw_a2, rw_g1, rw_g2, rw_k_k, rw_k_a, rw_r_k, rw_w_r, rw_w_k, rw_w_v, rw_w_o, rw_ln_w, rw_ln_b, peer_wq, peer_k1, peer_k2, peer_u, peer_v):
    bp, tp, d = x_prompt.shape
    bs = x_sample.shape[0]
    depth = ada_w.shape[0]
    d_att = H_ATT * HEAD_DIM
    groups = (_Group(bp, tp, 512), _Group(bs, 1, 128))
    xs = [x_prompt.reshape(bp * tp, d), x_sample.reshape(bs, d)]

    c_all = jnp.concatenate([c_prompt, c_sample], axis=0)
    n_seq = c_all.shape[0]
    c_all = jnp.pad(c_all, ((0, (-n_seq) % SUBLANES), (0, 0)))

    outs = {name: [[], []] for name in ("k", "v", "lf", "wkv", "shift")}
    for l in range(depth):
        mod_all = _ada_mod(c_all, ada_w[l], ada_b[l])[:n_seq].reshape(n_seq, N_MOD, d)
        mods = (mod_all[:bp], mod_all[bp:])
        i = l // 2
        if l % 2 == 0:
            w_qkv = att_w_in[i][:, :3 * d_att].astype(BF16)
            w_f = _pad_cols(att_w_in[i][:, 3 * d_att:], LANES).astype(BF16)
            f_bias = _pad_cols(att_f_bias[i][None, :], LANES)
            w_o = att_w_o[i].astype(BF16)
            for gi, grp in enumerate(groups):
                x, mod = xs[gi], mods[gi]
                h = _normmod(grp, x, norm_mix[l], mod[:, 0], mod[:, 1])
                q, k, v, logf = _att_project(grp, h, w_qkv, w_f, att_q_gain[i], att_k_gain[i], f_bias)
                if grp.seq > 1:
                    fcum = _cumsum_seq(logf.reshape(grp.batch, grp.seq, LANES))
                    fcum_t = jnp.swapaxes(fcum[:, :, :H_FOX], 1, 2)
                    o = _attn_prompt(q, k, v, fcum, fcum_t, grp.batch, grp.seq)
                else:
                    pool, page = cache_k.shape[1], cache_k.shape[2]
                    o = _attn_sample(q, k, v, logf,
                                     cache_k[i].reshape(pool, page, d_att), cache_v[i].reshape(pool, page, d_att),
                                     jnp.swapaxes(cache_logf[i], 1, 2), page_table)
                xs[gi] = _proj_residual(grp, o, w_o, x, mod[:, 2], "att_out")
                outs["k"][gi].append(k.reshape(grp.batch, grp.seq, H_ATT, HEAD_DIM))
                outs["v"][gi].append(v.reshape(grp.batch, grp.seq, H_ATT, HEAD_DIM))
                outs["lf"][gi].append(logf[:, :H_FOX].reshape(grp.batch, grp.seq, H_FOX))
        else:
            r_decay, r_aaa = rw_w1.shape[2], rw_a1.shape[2]
            w_rkv = jnp.concatenate([rw_w_r[i], rw_w_k[i], rw_w_v[i]], axis=1).astype(BF16)
            w_lr1 = jnp.concatenate([_pad_cols(rw_w1[i], LANES), _pad_cols(rw_a1[i], LANES), rw_g1[i]],
                                    axis=1).astype(BF16)
            n1 = w_lr1.shape[1]
            w_lr2 = jnp.zeros((n1, 3 * d), F32)
            w_lr2 = w_lr2.at[:r_decay, :d].set(rw_w2[i])
            w_lr2 = w_lr2.at[LANES:LANES + r_aaa, d:2 * d].set(rw_a2[i])
            w_lr2 = w_lr2.at[2 * LANES:, 2 * d:].set(rw_g2[i]).astype(BF16)
            bias_lr2 = jnp.concatenate([rw_w0[i], rw_a0[i], jnp.zeros((d,), F32)])[None, :]
            params = _pad_rows(jnp.stack([rw_k_k[i], rw_k_a[i], rw_r_k[i].reshape(d), rw_ln_w[i], rw_ln_b[i]]),
                               SUBLANES)
            w_o = rw_w_o[i].astype(BF16)
            for gi, grp in enumerate(groups):
                x, mod = xs[gi], mods[gi]
                h = _normmod(grp, x, norm_mix[l], mod[:, 0], mod[:, 1])
                h3 = h.reshape(grp.batch, grp.seq, d)
                if grp.seq > 1:
                    prev = jnp.concatenate([jnp.zeros((grp.batch, 1, d), F32), h3[:, :-1]], axis=1)
                else:
                    prev = state_shift[i][:, None, :]
                rkv, lag = _rwkv_project(grp, h, prev.reshape(grp.m, d), rw_mix[i], w_rkv, w_lr1, w_lr2, bias_lr2)
                if grp.seq > 1:
                    state0 = jnp.zeros((grp.batch, d // RWKV_HS, RWKV_HS, RWKV_HS), F32)
                    seq3 = lambda a: a.reshape(grp.batch, grp.seq, 3 * d)
                    z, s_fin = _rwkv_scan(seq3(rkv), seq3(lag), params, state0, grp.batch, grp.seq,
                                          min(64, grp.seq), 1)
                    z = z.reshape(grp.m, d)
                else:
                    z, s_fin = _rwkv_step(rkv, lag, params, state_wkv[i])
                xs[gi] = _proj_residual(grp, z, w_o, x, mod[:, 2], "rwkv_out")
                outs["wkv"][gi].append(s_fin)
                outs["shift"][gi].append(h3[:, -1])
        uv = jnp.concatenate([peer_u[l], peer_v[l]], axis=1)
        w_q = peer_wq[l].astype(BF16)
        for gi, grp in enumerate(groups):
            xs[gi] = _peer(grp, xs[gi], mods[gi], norm_ffn[l], w_q, peer_k1[l], peer_k2[l], uv)

    st = lambda name, gi: jnp.stack(outs[name][gi])
    return (xs[0].reshape(bp, tp, d), xs[1].reshape(bs, 1, d),
            st("k", 0), st("v", 0), st("lf", 0), st("wkv", 0), st("shift", 0),
            st("k", 1), st("v", 1), st("lf", 1), st("wkv", 1), st("shift", 1))
```

```python
import functools

import jax
import jax.numpy as jnp
from jax import lax
from jax.experimental import pallas as pl
from jax.experimental.pallas import tpu as pltpu

F32 = jnp.float32
BF16 = jnp.bfloat16

HEAD_DIM = 128
H_FOX = 8
H_ATT = 16
ATT_SCALE = HEAD_DIM ** -0.5
NEG_INF = -1e30
RWKV_HS = 64
LNX_EPS = 64e-5
PEER_HEADS = 8
PEER_TOPK = 16
N_KEYS = 128
N_MOD = 6
RMS_EPS = 1e-6

LANES = 128
SUBLANES = 8
VMEM_LIMIT_BYTES = 56 * 1024 * 1024


def _cparams(sem):
    return pltpu.CompilerParams(dimension_semantics=sem, vmem_limit_bytes=VMEM_LIMIT_BYTES)


def _split_bf16(x, parts):
    out = []
    for _ in range(parts - 1):
        hi = x.astype(BF16)
        out.append(hi)
        x = x - hi.astype(F32)
    out.append(x.astype(BF16))
    return out


def _dot_exact_rhs(x, rhs_bf16, parts=3):
    acc = None
    for p in _split_bf16(x, parts):
        t = jnp.dot(p, rhs_bf16, preferred_element_type=F32)
        acc = t if acc is None else acc + t
    return acc


def _dot_exact_lhs(lhs_bf16, x, parts=3):
    acc = None
    for p in _split_bf16(x, parts):
        t = jnp.dot(lhs_bf16, p, preferred_element_type=F32)
        acc = t if acc is None else acc + t
    return acc


def _dotf(a, b):
    a3 = _split_bf16(a, 3)
    b3 = _split_bf16(b, 3)
    acc = None
    for i, j in ((2, 0), (0, 2), (1, 1), (1, 0), (0, 1), (0, 0)):
        t = jnp.dot(a3[i], b3[j], preferred_element_type=F32)
        acc = t if acc is None else acc + t
    return acc


def _dotf_nt(a, b):
    a3 = _split_bf16(a, 3)
    b3 = _split_bf16(b, 3)
    acc = None
    dn = (((1,), (1,)), ((), ()))
    for i, j in ((2, 0), (0, 2), (1, 1), (1, 0), (0, 1), (0, 0)):
        t = lax.dot_general(a3[i], b3[j], dn, preferred_element_type=F32)
        acc = t if acc is None else acc + t
    return acc


def _softplus(z):
    return jnp.maximum(z, 0.0) + jnp.log1p(jnp.exp(-jnp.abs(z)))


def _sigmoid(z):
    return 1.0 / (1.0 + jnp.exp(-z))


class _Group:
    def __init__(self, batch, seq, tm):
        self.batch, self.seq = batch, seq
        self.m = batch * seq
        if seq == 1:
            self.tm = min(tm, self.m)
        else:
            self.tm = min(tm, seq)
        assert self.m % self.tm == 0 and (seq == 1 or seq % self.tm == 0)
        self.tiles_per_seq = max(seq // self.tm, 1)

    def seq_vec(self, v):
        return v if self.seq == 1 else v[:, None, :]

    def seq_spec(self, width, col=None):
        col = col or (lambda *ids: 0)
        if self.seq == 1:
            return pl.BlockSpec((self.tm, width), lambda *ids: (ids[0], col(*ids)))
        tps = self.tiles_per_seq
        return pl.BlockSpec((None, 1, width), lambda *ids: (ids[0] // tps, 0, col(*ids)))


def _normmod_body(x_ref, g_ref, shift_ref, scale_ref, o_ref):
    x = x_ref[...]
    h = x * lax.rsqrt(jnp.mean(x * x, axis=-1, keepdims=True) + RMS_EPS) * g_ref[...]
    o_ref[...] = h * (1.0 + scale_ref[...]) + shift_ref[...]


def _normmod(grp, x, g, shift, scale):
    m, d = x.shape
    return pl.pallas_call(
        _normmod_body,
        grid=(m // grp.tm,),
        in_specs=[pl.BlockSpec((grp.tm, d), lambda i: (i, 0)),
                  pl.BlockSpec((1, d), lambda i: (0, 0)),
                  grp.seq_spec(d), grp.seq_spec(d)],
        out_specs=pl.BlockSpec((grp.tm, d), lambda i: (i, 0)),
        out_shape=jax.ShapeDtypeStruct((m, d), F32),
        compiler_params=_cparams(("parallel",)),
        name="normmod",
    )(x, g[None, :], grp.seq_vec(shift), grp.seq_vec(scale))


def _mm_body(*refs, n_a, n_e, n_out, prologue, epilogue, every_j):
    a_refs = refs[:n_a]
    w_ref = refs[n_a]
    e_refs = refs[n_a + 1:n_a + 1 + n_e]
    o_refs = refs[n_a + 1 + n_e:n_a + 1 + n_e + n_out]
    a_scr = refs[-1]
    j = pl.program_id(1)

    def fill():
        a_scr[...] = prologue(*[r[...] for r in a_refs]).astype(BF16)

    if every_j:
        fill()
    else:
        pl.when(j == 0)(fill)
    acc = jnp.dot(a_scr[...], w_ref[...].astype(BF16), preferred_element_type=F32)
    epilogue(acc, j, e_refs, o_refs)


def _mm(m, tm, w, tn, a_ins, a_specs, e_ins, e_specs, out_shapes, out_specs, prologue, epilogue, name,
        every_j=False):
    k, n = w.shape
    assert m % tm == 0 and n % tn == 0
    body = functools.partial(_mm_body, n_a=len(a_ins), n_e=len(e_ins), n_out=len(out_shapes),
                             prologue=prologue, epilogue=epilogue, every_j=every_j)
    return pl.pallas_call(
        body,
        grid=(m // tm, n // tn),
        in_specs=list(a_specs) + [pl.BlockSpec((k, tn), lambda i, j: (0, j))] + list(e_specs),
        out_specs=list(out_specs),
        out_shape=list(out_shapes),
        scratch_shapes=[pltpu.VMEM((tm, k), BF16)],
        compiler_params=_cparams(("parallel", "arbitrary")),
        name=name,
    )(*a_ins, w, *e_ins)


def _tile_spec(tm, width):
    return pl.BlockSpec((tm, width), lambda i, j: (i, 0))


def _out_spec(tm, tn):
    return pl.BlockSpec((tm, tn), lambda i, j: (i, j))


def _row_spec(tn):
    return pl.BlockSpec((1, tn), lambda i, j: (0, j))


def _ident(a):
    return a


def _epi(fn):
    def run(acc, j, e_refs, o_refs):
        o_refs[0][...] = fn(acc, j, *[r[...] for r in e_refs])
    return run


def _ada_mod(c, w, b):
    m, d = c.shape
    n = w.shape[1]
    tn = 512
    (out,) = _mm(m, m, w, tn, [c], [_tile_spec(m, d)], [b[None, :]], [_row_spec(tn)],
                 [jax.ShapeDtypeStruct((m, n), F32)], [_out_spec(m, tn)],
                 lambda cc: cc * _sigmoid(cc), _epi(lambda acc, j, bb: acc + bb), "ada_mod")
    return out


def _head_rmsnorm(y, gain):
    blocks = []
    for hh in range(y.shape[1] // HEAD_DIM):
        blk = y[:, hh * HEAD_DIM:(hh + 1) * HEAD_DIM]
        ms = jnp.mean(blk * blk, axis=-1, keepdims=True)
        blocks.append(blk * lax.rsqrt(ms + RMS_EPS) * gain)
    return jnp.concatenate(blocks, axis=1)


def _qkv_epilogue(acc, j, e_refs, o_refs):
    gains = e_refs[0]
    q_ref, k_ref, v_ref = o_refs

    @pl.when(j == 0)
    def _():
        q_ref[...] = _head_rmsnorm(acc, gains[0:1, :])

    @pl.when(j == 1)
    def _():
        q_ref[...] = acc

    @pl.when(j == 2)
    def _():
        k_ref[...] = _head_rmsnorm(acc, gains[1:2, :])

    @pl.when(j == 3)
    def _():
        k_ref[...] = acc

    @pl.when(j >= 4)
    def _():
        v_ref[...] = acc


def _att_project(grp, h, w_qkv, w_f, q_gain, k_gain, f_bias):
    m, d = h.shape
    tm = grp.tm
    d_att = H_ATT * HEAD_DIM
    tn = d_att // 2
    gains = jnp.stack([q_gain, k_gain])
    shp = jax.ShapeDtypeStruct((m, d_att), F32)

    def ospec(first):
        return pl.BlockSpec((tm, tn), lambda i, j: (i, jnp.clip(j - first, 0, 1)))

    q, k, v = _mm(m, tm, w_qkv, tn, [h], [_tile_spec(tm, d)],
                  [gains], [pl.BlockSpec((2, HEAD_DIM), lambda i, j: (0, 0))],
                  [shp, shp, shp], [ospec(0), ospec(2), ospec(4)],
                  _ident, _qkv_epilogue, "att_qkv")
    (logf,) = _mm(m, tm, w_f, LANES, [h], [_tile_spec(tm, d)],
                  [f_bias], [_row_spec(LANES)],
                  [jax.ShapeDtypeStruct((m, LANES), F32)], [_out_spec(tm, LANES)],
                  _ident, _epi(lambda acc, j, fb: -_softplus(-(acc + fb))), "att_logf")
    return q, k, v, logf


def _cumsum_body(x_ref, o_ref):
    t = x_ref.shape[0]
    row = lax.broadcasted_iota(jnp.int32, (t, t), 0)
    col = lax.broadcasted_iota(jnp.int32, (t, t), 1)
    tri = (col <= row).astype(BF16)
    o_ref[...] = _dot_exact_lhs(tri, x_ref[...])


def _cumsum_seq(x):
    b, t, w = x.shape
    return pl.pallas_call(
        _cumsum_body,
        grid=(b,),
        in_specs=[pl.BlockSpec((None, t, w), lambda i: (i, 0, 0))],
        out_specs=pl.BlockSpec((None, t, w), lambda i: (i, 0, 0)),
        out_shape=jax.ShapeDtypeStruct((b, t, w), F32),
        compiler_params=_cparams(("parallel",)),
        name="fox_cumsum",
    )(x)


def _attn_prompt_body(q_ref, k_ref, v_ref, fc_ref, ft_ref, o_ref, *, tq):
    h = pl.program_id(1)
    qi = pl.program_id(2)
    q = q_ref[...].astype(BF16)
    t_pos = qi * tq + lax.broadcasted_iota(jnp.int32, (tq, tq), 0)
    s_off = lax.broadcasted_iota(jnp.int32, (tq, tq), 1)
    nt = (((1,), (1,)), ((), ()))

    def scores(kb):
        start = pl.multiple_of(kb * tq, tq)
        kblk = k_ref[pl.ds(start, tq), :].astype(BF16)
        vblk = v_ref[pl.ds(start, tq), :].astype(BF16)
        s = lax.dot_general(q, kblk, nt, preferred_element_type=F32) * ATT_SCALE
        return s, vblk, start

    @pl.when(h < H_FOX)
    def _fox():
        lane = lax.broadcasted_iota(jnp.int32, fc_ref.shape, 1)
        fq = jnp.sum(jnp.where(lane == h, fc_ref[...], 0.0), axis=-1, keepdims=True)

        def step(kb, carry):
            m_run, l_run, acc = carry
            s, vblk, start = scores(kb)
            fk = ft_ref[pl.ds(h, 1), pl.ds(start, tq)]
            s = s + (fq - fk)
            s = jnp.where(kb * tq + s_off <= t_pos, s, NEG_INF)
            m_new = jnp.maximum(m_run, jnp.max(s, axis=-1, keepdims=True))
            alpha = jnp.exp(m_run - m_new)
            p = jnp.exp(s - m_new)
            l_new = alpha * l_run + jnp.sum(p, axis=-1, keepdims=True)
            acc = alpha * acc + jnp.dot(p.astype(BF16), vblk, preferred_element_type=F32)
            return m_new, l_new, acc

        init = (jnp.full((tq, 1), NEG_INF, F32), jnp.zeros((tq, 1), F32), jnp.zeros((tq, HEAD_DIM), F32))
        _, l_fin, acc = lax.fori_loop(0, qi + 1, step, init)
        o_ref[...] = acc / l_fin

    @pl.when(h >= H_FOX)
    def _sb():
        jj = lax.broadcasted_iota(jnp.int32, (tq, tq), 0)
        ss = lax.broadcasted_iota(jnp.int32, (tq, tq), 1)
        upper = (jj > ss).astype(BF16)

        def step(it, carry):
            later, acc = carry
            kb = qi - it
            z, vblk, _ = scores(kb)
            strict = kb * tq + s_off < t_pos
            lk = jnp.where(strict, -_softplus(z), 0.0)
            later_in = _dot_exact_rhs(lk, upper, parts=2)
            a = jnp.where(strict, jnp.exp(z + lk + later_in + later), 0.0)
            acc = acc + jnp.dot(a.astype(BF16), vblk, preferred_element_type=F32)
            later = later + jnp.sum(lk, axis=-1, keepdims=True)
            return later, acc

        init = (jnp.zeros((tq, 1), F32), jnp.zeros((tq, HEAD_DIM), F32))
        _, acc = lax.fori_loop(0, qi + 1, step, init)
        o_ref[...] = acc


def _attn_prompt(q, k, v, fcum, fcum_t, batch, seq):
    d_att = q.shape[1]
    tq = min(256, seq)
    q3, k3, v3 = (a.reshape(batch, seq, d_att) for a in (q, k, v))
    body = functools.partial(_attn_prompt_body, tq=tq)
    out = pl.pallas_call(
        body,
        grid=(batch, H_ATT, seq // tq),
        in_specs=[pl.BlockSpec((None, tq, HEAD_DIM), lambda b, h, i: (b, i, h)),
                  pl.BlockSpec((None, seq, HEAD_DIM), lambda b, h, i: (b, 0, h)),
                  pl.BlockSpec((None, seq, HEAD_DIM), lambda b, h, i: (b, 0, h)),
                  pl.BlockSpec((None, tq, LANES), lambda b, h, i: (b, i, 0)),
                  pl.BlockSpec((None, H_FOX, seq), lambda b, h, i: (b, 0, 0))],
        out_specs=pl.BlockSpec((None, tq, HEAD_DIM), lambda b, h, i: (b, i, h)),
        out_shape=jax.ShapeDtypeStruct((batch, seq, d_att), F32),
        compiler_params=_cparams(("parallel", "parallel", "arbitrary")),
        name="attn_prompt",
    )(q3, k3, v3, fcum, fcum_t)
    return out.reshape(batch * seq, d_att)


def _attn_sample_body(pt_ref, q_ref, kn_ref, vn_ref, lfn_ref, kc_ref, vc_ref, lfc_ref, o_ref,
                      qrows, m_scr, l_scr, suf_scr, acc_scr, *, n_pages):
    p = pl.program_id(1)
    d_att = q_ref.shape[-1]
    page = kc_ref.shape[0]
    sub = lax.broadcasted_iota(jnp.int32, (H_ATT, d_att), 0)
    lane_head = lax.broadcasted_iota(jnp.int32, (H_ATT, d_att), 1) // HEAD_DIM
    own = sub == lane_head
    is_fox = lax.broadcasted_iota(jnp.int32, (H_ATT, 1), 0) < H_FOX
    nt = (((1,), (1,)), ((), ()))

    @pl.when(p == 0)
    def _init():
        qr = jnp.where(own, q_ref[...], 0.0)
        qrows[...] = qr.astype(BF16)
        s_self = jnp.sum(qr * kn_ref[...], axis=-1, keepdims=True) * ATT_SCALE
        m_scr[...] = jnp.where(is_fox, s_self, 0.0)
        l_scr[...] = jnp.ones_like(l_scr)
        acc_scr[...] = jnp.where(sub < H_FOX, jnp.broadcast_to(vn_ref[...], (H_ATT, d_att)), 0.0)
        lf_lane = lax.broadcasted_iota(jnp.int32, (H_ATT, LANES), 1)
        lf_sub = lax.broadcasted_iota(jnp.int32, (H_ATT, LANES), 0)
        lf_new = jnp.sum(jnp.where(lf_lane == lf_sub, lfn_ref[...], 0.0), axis=-1, keepdims=True)
        suf_scr[...] = jnp.where(is_fox, lf_new, 0.0)

    s = lax.dot_general(qrows[...], kc_ref[...].astype(BF16), nt,
                        preferred_element_type=F32) * ATT_SCALE
    jj = lax.broadcasted_iota(jnp.int32, (page, page), 0)
    ss = lax.broadcasted_iota(jnp.int32, (page, page), 1)
    upper = (jj > ss).astype(BF16)
    lk = -_softplus(s)
    lf = jnp.concatenate([lfc_ref[...], jnp.zeros((H_ATT - H_FOX, page), F32)], axis=0)
    step_log = jnp.where(is_fox, lf, lk)
    later = _dot_exact_rhs(step_log, upper) + suf_scr[...]
    logit = s + later + jnp.where(is_fox, 0.0, lk)
    m_old = m_scr[...]
    m_new = jnp.where(is_fox, jnp.maximum(m_old, jnp.max(logit, axis=-1, keepdims=True)), 0.0)
    alpha = jnp.exp(m_old - m_new)
    w = jnp.exp(logit - m_new)
    l_scr[...] = jnp.where(is_fox, alpha * l_scr[...] + jnp.sum(w, axis=-1, keepdims=True), 1.0)
    acc_scr[...] = alpha * acc_scr[...] + jnp.dot(w.astype(BF16), vc_ref[...].astype(BF16),
                                                  preferred_element_type=F32)
    m_scr[...] = m_new
    suf_scr[...] = suf_scr[...] + jnp.sum(step_log, axis=-1, keepdims=True)

    @pl.when(p == n_pages - 1)
    def _fin():
        full = acc_scr[...] / l_scr[...]
        o_ref[...] = jnp.sum(jnp.where(own, full, 0.0), axis=0, keepdims=True)


def _attn_sample(q, k_new, v_new, logf_new, cache_k, cache_v, cache_lf_t, page_table):
    b, d_att = q.shape
    n_pages = page_table.shape[1]
    page = cache_k.shape[1]
    row = lambda a: a[:, None, :]

    def new_spec(width):
        return pl.BlockSpec((None, 1, width), lambda i, p, pt: (i, 0, 0))

    def page_spec(shape):
        return pl.BlockSpec((None,) + shape, lambda i, p, pt: (pt[i, n_pages - 1 - p], 0, 0))

    grid_spec = pltpu.PrefetchScalarGridSpec(
        num_scalar_prefetch=1,
        grid=(b, n_pages),
        in_specs=[new_spec(d_att), new_spec(d_att), new_spec(d_att), new_spec(LANES),
                  page_spec((page, d_att)), page_spec((page, d_att)), page_spec((H_FOX, page))],
        out_specs=pl.BlockSpec((None, 1, d_att), lambda i, p, pt: (i, 0, 0)),
        scratch_shapes=[pltpu.VMEM((H_ATT, d_att), BF16), pltpu.VMEM((H_ATT, 1), F32),
                        pltpu.VMEM((H_ATT, 1), F32), pltpu.VMEM((H_ATT, 1), F32),
                        pltpu.VMEM((H_ATT, d_att), F32)],
    )
    out = pl.pallas_call(
        functools.partial(_attn_sample_body, n_pages=n_pages),
        grid_spec=grid_spec,
        out_shape=jax.ShapeDtypeStruct((b, 1, d_att), F32),
        compiler_params=_cparams(("parallel", "arbitrary")),
        name="attn_sample",
    )(page_table, row(q), row(k_new), row(v_new), row(logf_new), cache_k, cache_v, cache_lf_t)
    return out.reshape(b, d_att)


def _proj_residual(grp, a, w, x, gate, name):
    m, d = x.shape
    tm, tn = grp.tm, min(1024, d)
    (out,) = _mm(m, tm, w, tn, [a], [_tile_spec(tm, a.shape[1])],
                 [x, grp.seq_vec(gate)], [_out_spec(tm, tn), grp.seq_spec(tn, col=lambda i, j: j)],
                 [jax.ShapeDtypeStruct((m, d), F32)], [_out_spec(tm, tn)],
                 _ident, _epi(lambda acc, j, xx, gg: xx + gg * acc), name)
    return out


def _mix_prologue(h, prev, mix):
    return h + (prev - h) * mix


def _rwkv_project(grp, h, prev, mix, w_rkv, w_lr1, w_lr2, bias_lr2):
    m, d = h.shape
    tm = grp.tm
    tn = d // 2
    mix_rkv = jnp.stack([mix[0], mix[2], mix[3]])[:, None, :]
    (rkv,) = _mm(m, tm, w_rkv, tn, [h, prev, mix_rkv],
                 [_tile_spec(tm, d), _tile_spec(tm, d),
                  pl.BlockSpec((None, 1, d), lambda i, j: (j // 2, 0, 0))],
                 [], [], [jax.ShapeDtypeStruct((m, 3 * d), F32)], [_out_spec(tm, tn)],
                 _mix_prologue, _epi(lambda acc, j: acc), "rwkv_rkv", every_j=True)

    mix_lr = jnp.stack([mix[1], mix[4], mix[5], mix[5]])[:, None, :]

    def lr1_act(acc, j):
        return lax.cond(j == 0, jnp.tanh, lambda a: lax.cond(j == 1, _ident, _sigmoid, a), acc)

    n1 = w_lr1.shape[1]
    (lr1,) = _mm(m, tm, w_lr1, LANES, [h, prev, mix_lr],
                 [_tile_spec(tm, d), _tile_spec(tm, d),
                  pl.BlockSpec((None, 1, d), lambda i, j: (j, 0, 0))],
                 [], [], [jax.ShapeDtypeStruct((m, n1), F32)], [_out_spec(tm, LANES)],
                 _mix_prologue, _epi(lr1_act), "rwkv_lr1", every_j=True)

    def lr2_act(acc, j, bias):
        y = acc + bias
        decay = lambda a: -jnp.exp(-_softplus(-a) - 0.5)
        return lax.cond(j < 2, decay, lambda a: lax.cond(j < 4, _sigmoid, _ident, a), y)

    (lag,) = _mm(m, tm, w_lr2, tn, [lr1], [_tile_spec(tm, n1)],
                 [bias_lr2], [_row_spec(tn)],
                 [jax.ShapeDtypeStruct((m, 3 * d), F32)], [_out_spec(tm, tn)],
                 _ident, _epi(lr2_act), "rwkv_lr2")
    return rkv, lag


def _dot3(a, b, dn):
    return lax.dot_general(a.astype(BF16), b.astype(BF16), dn, preferred_element_type=F32)


_NN = (((1,), (0,)), ((), ()))
_NT = (((1,), (1,)), ((), ()))


def _half_sum(x, lo_half):
    s0 = jnp.sum(jnp.where(lo_half, x, 0.0), axis=-1, keepdims=True)
    s1 = jnp.sum(jnp.where(lo_half, 0.0, x), axis=-1, keepdims=True)
    return jnp.where(lo_half, s0, s1)


def _rwkv_scan_body(r_ref, k_ref, v_ref, lw_ref, a_ref, g_ref, par_ref, s0_ref, z_ref, sT_ref,
                    s_scr, a2_scr, u0_scr, o0_scr, rh_scr, bp_scr, no_scr, kv_scr, last_scr,
                    *, chunk, n_chunks, n_pairs):
    hs = RWKV_HS
    blk = 2 * hs
    lane = lax.broadcasted_iota(jnp.int32, (1, blk), 1)
    lo_half = lane < hs
    hi_half = jnp.logical_not(lo_half)
    row = lax.broadcasted_iota(jnp.int32, (chunk, chunk), 0)
    col = lax.broadcasted_iota(jnp.int32, (chunk, chunk), 1)
    tril_incl = (col <= row)
    tril_strict = (col < row)
    eye = (col == row).astype(F32)
    rr = lax.broadcasted_iota(jnp.int32, (blk, blk), 0) < hs
    cc = lax.broadcasted_iota(jnp.int32, (blk, blk), 1) < hs
    same_head = rr == cc
    n_double = max(chunk.bit_length() - 2, 0)
    zero = jnp.zeros((hs, hs), F32)

    def lanes(pi):
        return slice(pi * blk, (pi + 1) * blk)

    def params(pi):
        return [par_ref[i:i + 1, lanes(pi)] for i in range(5)]

    def prepare(c, pi):
        sl = pl.ds(pl.multiple_of(c * chunk, chunk), chunk)
        r, k, v, lw, a = (ref[sl, lanes(pi)] for ref in (r_ref, k_ref, v_ref, lw_ref, a_ref))
        k_k, k_a = params(pi)[:2]
        kk = k * k_k
        kk = kk / jnp.maximum(jnp.sqrt(_half_sum(kk * kk, lo_half)), 1e-12)
        kmod = k * (1.0 + (a - 1.0) * k_a)
        cum = _dot_exact_lhs(tril_incl.astype(BF16), lw)
        dec_in = jnp.exp(cum)
        inv = jnp.exp(-cum)
        a_hat, b_hat, k_hat, r_hat = -kk * jnp.exp(cum - lw), kk * a * inv, kmod * inv, r * dec_in
        last = dec_in[chunk - 1:chunk, :]

        def solve(msk):
            am = jnp.where(msk, a_hat, 0.0)
            n_mat = jnp.where(tril_strict, _dot3(am, b_hat, _NT), 0.0)
            m_mat = jnp.where(tril_strict, _dot3(am, k_hat, _NT), 0.0)
            inv_mat, pw = eye + n_mat, n_mat
            for _ in range(n_double):
                pw = _dot3(pw, pw, _NN)
                inv_mat = inv_mat + _dot3(inv_mat, pw, _NN)
            return _dot3(inv_mat, a_hat, _NN), _dot3(inv_mat, _dot3(m_mat, v, _NN), _NN)

        a2_lo, u0_lo = solve(lo_half)
        a2_hi, u0_hi = solve(hi_half)
        a2_scr[pi, c] = jnp.where(lo_half, a2_lo, a2_hi)
        u0_scr[pi, c] = jnp.where(lo_half, u0_lo, u0_hi)

        def readout(msk):
            rm = jnp.where(msk, r_hat, 0.0)
            no = jnp.where(tril_incl, _dot3(rm, b_hat, _NT), 0.0)
            mo = jnp.where(tril_incl, _dot3(rm, k_hat, _NT), 0.0)
            return no, _dot3(mo, v, _NN)

        no_lo, o0_lo = readout(lo_half)
        no_hi, o0_hi = readout(hi_half)
        no_scr[pi, c, 0] = no_lo
        no_scr[pi, c, 1] = no_hi
        o0_scr[pi, c] = jnp.where(lo_half, o0_lo, o0_hi)
        rh_scr[pi, c] = r_hat
        bp_scr[pi, c] = b_hat * last
        kv_scr[pi, c] = jnp.where(same_head, _dot3(v.T, k_hat * last, _NN), 0.0)
        last_scr[pi, c] = jnp.broadcast_to(last, (SUBLANES, blk))

    def phase1(c, _):
        for pi in range(n_pairs):
            prepare(c, pi)
        return 0

    lax.fori_loop(0, n_chunks, phase1, 0, unroll=2 if n_chunks % 2 == 0 else 1)

    for pi in range(n_pairs):
        s_scr[pi] = jnp.concatenate([jnp.concatenate([s0_ref[2 * pi], zero], axis=1),
                                     jnp.concatenate([zero, s0_ref[2 * pi + 1]], axis=1)], axis=0)

    def advance(c, pi):
        sl = pl.ds(pl.multiple_of(c * chunk, chunk), chunk)
        state = s_scr[pi]
        last = last_scr[pi, c][0:1, :]
        u = u0_scr[pi, c] + _dot3(a2_scr[pi, c], state, _NT)
        s_scr[pi] = jnp.where(same_head, state * last + _dot3(u.T, bp_scr[pi, c], _NN), 0.0) + kv_scr[pi, c]
        o = (_dot3(rh_scr[pi, c], state, _NT) + o0_scr[pi, c]
             + jnp.where(lo_half, _dot3(no_scr[pi, c, 0], u, _NN), _dot3(no_scr[pi, c, 1], u, _NN)))

        r, k, v, a, g = (ref[sl, lanes(pi)] for ref in (r_ref, k_ref, v_ref, a_ref, g_ref))
        _, k_a, r_k, ln_w, ln_b = params(pi)
        kmod = k * (1.0 + (a - 1.0) * k_a)
        mu = _half_sum(o, lo_half) / hs
        dev = o - mu
        var = _half_sum(dev * dev, lo_half) / hs
        o_n = dev * lax.rsqrt(var + LNX_EPS) * ln_w + ln_b
        bonus = _half_sum(r * kmod * r_k, lo_half) * v
        z_ref[sl, lanes(pi)] = (o_n + bonus) * g

    def phase2(c, _):
        for pi in range(n_pairs):
            advance(c, pi)
        return 0

    lax.fori_loop(0, n_chunks, phase2, 0)
    for pi in range(n_pairs):
        fin = s_scr[pi]
        sT_ref[2 * pi] = fin[:hs, :hs]
        sT_ref[2 * pi + 1] = fin[hs:, hs:]


def _rwkv_scan(rkv, lag, params, state0, batch, seq, chunk, n_pairs):
    d = rkv.shape[-1] // 3
    blk = 2 * RWKV_HS
    width = n_pairs * blk
    groups = d // width
    n_chunks = seq // chunk

    def seq_spec(part):
        return pl.BlockSpec((None, seq, width), lambda b, p: (b, 0, part * groups + p))

    st_spec = pl.BlockSpec((None, 2 * n_pairs, RWKV_HS, RWKV_HS), lambda b, p: (b, p, 0, 0))
    body = functools.partial(_rwkv_scan_body, chunk=chunk, n_chunks=n_chunks, n_pairs=n_pairs)
    per_chunk = lambda *shape: pltpu.VMEM((n_pairs, n_chunks) + shape, F32)
    z, s_fin = pl.pallas_call(
        body,
        grid=(batch, groups),
        in_specs=[seq_spec(0), seq_spec(1), seq_spec(2), seq_spec(0), seq_spec(1), seq_spec(2),
                  pl.BlockSpec((SUBLANES, width), lambda b, p: (0, p)), st_spec],
        out_specs=[pl.BlockSpec((None, seq, width), lambda b, p: (b, 0, p)), st_spec],
        out_shape=[jax.ShapeDtypeStruct((batch, seq, d), F32),
                   jax.ShapeDtypeStruct(state0.shape, F32)],
        scratch_shapes=[pltpu.VMEM((n_pairs, blk, blk), F32),
                        per_chunk(chunk, blk), per_chunk(chunk, blk), per_chunk(chunk, blk),
                        per_chunk(chunk, blk), per_chunk(chunk, blk), per_chunk(2, chunk, chunk),
                        per_chunk(blk, blk), per_chunk(SUBLANES, blk)],
        compiler_params=_cparams(("parallel", "parallel")),
        name="rwkv_scan",
    )(rkv, rkv, rkv, lag, lag, lag, params, state0)
    return z, s_fin


def _rwkv_step_body(rkv_ref, lag_ref, par_ref, s0_ref, z_ref, s1_ref):
    hs = RWKV_HS
    r, k, v = rkv_ref[0], rkv_ref[1], rkv_ref[2]
    lw, a, g = lag_ref[0], lag_ref[1], lag_ref[2]
    k_k, k_a, r_k, ln_w, ln_b = (par_ref[i] for i in range(5))
    eye = (lax.broadcasted_iota(jnp.int32, (hs, hs), 0) == lax.broadcasted_iota(jnp.int32, (hs, hs), 1))
    kk = k * k_k
    kk = kk / jnp.maximum(jnp.sqrt(jnp.sum(kk * kk, axis=-1, keepdims=True)), 1e-12)
    kmod = k * (1.0 + (a - 1.0) * k_a)
    state = s0_ref[...]
    sa = jnp.sum(state * (-kk), axis=-1, keepdims=True)
    v_col = jnp.sum(jnp.where(eye, v, 0.0), axis=-1, keepdims=True)
    new = state * jnp.exp(lw) + sa * (kk * a) + v_col * kmod
    s1_ref[...] = new
    o_col = jnp.sum(new * r, axis=-1, keepdims=True)
    o = jnp.sum(jnp.where(eye, o_col, 0.0), axis=1, keepdims=True)
    mu = jnp.mean(o, axis=-1, keepdims=True)
    dev = o - mu
    var = jnp.mean(dev * dev, axis=-1, keepdims=True)
    o_n = dev * lax.rsqrt(var + LNX_EPS) * ln_w + ln_b
    bonus = jnp.sum(r * kmod * r_k, axis=-1, keepdims=True) * v
    z_ref[...] = (o_n + bonus) * g


def _rwkv_step(rkv, lag, params, state0):
    b, heads, hs, _ = state0.shape
    vec = lambda x: x.reshape(b, 3, heads, 1, hs)
    vec_spec = pl.BlockSpec((None, 3, heads, 1, hs), lambda i: (i, 0, 0, 0, 0))
    st_spec = pl.BlockSpec((None, heads, hs, hs), lambda i: (i, 0, 0, 0))
    z, s_fin = pl.pallas_call(
        _rwkv_step_body,
        grid=(b,),
        in_specs=[vec_spec, vec_spec, pl.BlockSpec((SUBLANES, heads, 1, hs), lambda i: (0, 0, 0, 0)), st_spec],
        out_specs=[pl.BlockSpec((None, heads, 1, hs), lambda i: (i, 0, 0, 0)), st_spec],
        out_shape=[jax.ShapeDtypeStruct((b, heads, 1, hs), F32), jax.ShapeDtypeStruct(state0.shape, F32)],
        compiler_params=_cparams(("parallel",)),
        name="rwkv_step",
    )(vec(rkv), vec(lag), params.reshape(SUBLANES, heads, 1, hs), state0)
    return z.reshape(b, heads * hs), s_fin


def _peer_route_body(q_ref, k1_ref, k2_ref, ids_ref, gate_ref):
    tm = q_ref.shape[0]
    topk = PEER_TOPK
    n_cand = topk * topk
    half = k1_ref.shape[-1]
    key_row = lax.broadcasted_iota(jnp.int32, (N_KEYS, tm), 0).astype(F32)
    cand_row = lax.broadcasted_iota(jnp.int32, (n_cand, tm), 0).astype(F32)
    rank = lax.broadcasted_iota(jnp.int32, (topk, tm), 0)
    rank_f = rank.astype(F32)

    def take_max(s, rows, n_rows):
        m = jnp.max(s, axis=0, keepdims=True)
        idx = jnp.min(jnp.where(s == m, rows, float(n_rows)), axis=0, keepdims=True)
        return m, idx, jnp.where(rows == idx, -jnp.inf, s)

    vals_heads, ids_heads = [], []
    for hq in range(PEER_HEADS):
        q1 = q_ref[:, hq * 2 * half:hq * 2 * half + half].astype(BF16)
        q2 = q_ref[:, hq * 2 * half + half:(hq + 1) * 2 * half].astype(BF16)
        s1 = lax.dot_general(k1_ref[hq].astype(BF16), q1, _NT, preferred_element_type=F32)
        s2 = lax.dot_general(k2_ref[hq].astype(BF16), q2, _NT, preferred_element_type=F32)

        def sub_it(i, carry):
            s1, s2, v1, e1, v2, e2 = carry
            m1, i1, s1 = take_max(s1, key_row, N_KEYS)
            m2, i2, s2 = take_max(s2, key_row, N_KEYS)
            here = rank == i
            return (s1, s2, jnp.where(here, m1, v1), jnp.where(here, i1, e1),
                    jnp.where(here, m2, v2), jnp.where(here, i2, e2))

        zero = jnp.zeros((topk, tm), F32)
        _, _, v1, e1, v2, e2 = lax.fori_loop(0, topk, sub_it, (s1, s2, zero, zero, zero, zero))
        cs = jnp.concatenate([v1[i:i + 1, :] + v2 for i in range(topk)], axis=0)

        def cand_it(i, carry):
            cs, vals, ids = carry
            m, pos, cs = take_max(cs, cand_row, n_cand)
            hi = jnp.floor(pos * (1.0 / topk))
            lo = pos - hi * topk
            k_hi = jnp.sum(jnp.where(rank_f == hi, e1, 0.0), axis=0, keepdims=True)
            k_lo = jnp.sum(jnp.where(rank_f == lo, e2, 0.0), axis=0, keepdims=True)
            here = rank == i
            return cs, jnp.where(here, m, vals), jnp.where(here, k_hi * N_KEYS + k_lo, ids)

        _, vals, ids = lax.fori_loop(0, topk, cand_it, (cs, zero, zero))
        ex = jnp.exp(vals - jnp.max(vals, axis=0, keepdims=True))
        vals_heads.append(ex / jnp.sum(ex, axis=0, keepdims=True))
        ids_heads.append(ids)

    gate_ref[...] = jnp.concatenate(vals_heads, axis=0).T
    ids_ref[...] = jnp.concatenate(ids_heads, axis=0).T.astype(jnp.int32)


def _peer_route(q, k1, k2, tm):
    m, dq = q.shape
    assert PEER_HEADS * PEER_TOPK == N_KEYS
    kspec = pl.BlockSpec(k1.shape, lambda i: (0, 0, 0))
    ospec = pl.BlockSpec((tm, N_KEYS), lambda i: (i, 0))
    return pl.pallas_call(
        _peer_route_body,
        grid=(m // tm,),
        in_specs=[pl.BlockSpec((tm, dq), lambda i: (i, 0)), kspec, kspec],
        out_specs=[ospec, ospec],
        out_shape=[jax.ShapeDtypeStruct((m, N_KEYS), jnp.int32), jax.ShapeDtypeStruct((m, N_KEYS), F32)],
        compiler_params=_cparams(("parallel",)),
        name="peer_route",
    )(q, k1, k2)


def _gelu(x):
    return 0.5 * x * (1.0 + lax.erf(x * (2.0 ** -0.5)))


def _peer_expert_body(ids_ref, nxt_ref, h_ref, gate_ref, x_ref, g5_ref, uv_hbm, o_ref, buf0, buf1, sem, pout,
                      *, tt):
    d = h_ref.shape[1]
    n_sel = gate_ref.shape[1]
    step = pl.program_id(0)
    last_step = pl.num_programs(0) - 1
    bufs = (buf0, buf1)
    diag = (lax.broadcasted_iota(jnp.int32, (n_sel, n_sel), 0)
            == lax.broadcasted_iota(jnp.int32, (n_sel, n_sel), 1))

    n_rows = uv_hbm.shape[1] // 2
    seg = uv_hbm.shape[2]

    def issue(src_ref, row, slot):
        for p in range(n_sel):
            pltpu.make_async_copy(uv_hbm.at[src_ref[row, p]], bufs[slot].at[:, p], sem.at[slot]).start()

    def wait_all(slot):
        pltpu.make_async_copy(bufs[slot], bufs[slot], sem.at[slot]).wait()

    def compute(t, slot):
        part = None
        hrow = h_ref[pl.ds(t, 1), :]
        for r in range(n_rows):
            term = bufs[slot][r] * hrow[:, r * seg:(r + 1) * seg]
            part = term if part is None else part + term
        act = jnp.sum(part, axis=1, keepdims=True)
        gate = jnp.sum(jnp.where(diag, gate_ref[pl.ds(t, 1), :], 0.0), axis=1, keepdims=True)
        wgt = _gelu(act) * gate
        pout[pl.ds(t, 1), :] = jnp.concatenate(
            [jnp.sum(bufs[slot][n_rows + r] * wgt, axis=0, keepdims=True) for r in range(n_rows)], axis=1)

    @pl.when(step == 0)
    def _():
        issue(ids_ref, 0, 0)

    def token_pair(k, _):
        t0 = 2 * k
        wait_all(0)
        issue(ids_ref, t0 + 1, 1)
        compute(t0, 0)
        wait_all(1)
        issue(ids_ref, t0 + 2, 0)
        compute(t0 + 1, 1)
        return 0

    lax.fori_loop(0, tt // 2 - 1, token_pair, 0)
    wait_all(0)
    issue(ids_ref, tt - 1, 1)
    compute(tt - 2, 0)
    wait_all(1)
    issue(nxt_ref, 0, 0)
    compute(tt - 1, 1)

    @pl.when(step == last_step)
    def _():
        wait_all(0)

    o_ref[...] = x_ref[...] + g5_ref[...] * pout[...]


def _peer_experts(grp, h, ids, gates, x, gate5, uv, tt):
    m, d = h.shape
    n_sel = ids.shape[1]
    tt = min(tt, grp.tm)
    assert grp.tm % tt == 0 and tt % 2 == 0
    sub = grp.tm // tt
    n_steps = m // tt
    ids_spec = lambda nxt: pl.BlockSpec((tt, n_sel), lambda i: (jnp.minimum(i + nxt, n_steps - 1), 0),
                                        memory_space=pltpu.SMEM)
    tile = lambda width: pl.BlockSpec((tt, width), lambda i: (i, 0))
    if grp.seq == 1:
        g5, g5_spec = gate5, tile(d)
    else:
        tps = grp.tiles_per_seq * sub
        g5, g5_spec = gate5[:, None, :], pl.BlockSpec((None, 1, d), lambda i: (i // tps, 0, 0))
    return pl.pallas_call(
        functools.partial(_peer_expert_body, tt=tt),
        grid=(n_steps,),
        in_specs=[ids_spec(0), ids_spec(1),
                  tile(d), tile(n_sel), tile(d), g5_spec,
                  pl.BlockSpec(memory_space=pl.ANY)],
        out_specs=tile(d),
        out_shape=jax.ShapeDtypeStruct((m, d), F32),
        scratch_shapes=[pltpu.VMEM((uv.shape[1], n_sel, uv.shape[2]), F32),
                        pltpu.VMEM((uv.shape[1], n_sel, uv.shape[2]), F32),
                        pltpu.SemaphoreType.DMA((2,)), pltpu.VMEM((tt, d), F32)],
        compiler_params=_cparams(("arbitrary",)),
        name="peer_experts",
    )(ids, ids, h, gates, x, g5, uv)


def _peer(grp, x, mod, norm_g, w_q, k1, k2, uv):
    h = _normmod(grp, x, norm_g, mod[:, 3], mod[:, 4])
    m, d = h.shape
    tn = min(1024, w_q.shape[1])
    (q,) = _mm(m, grp.tm, w_q, tn, [h], [_tile_spec(grp.tm, d)], [], [],
               [jax.ShapeDtypeStruct((m, w_q.shape[1]), F32)], [_out_spec(grp.tm, tn)],
               _ident, _epi(lambda acc, j: acc), "peer_q")
    ids, gates = _peer_route(q, k1, k2, min(128, m))
    return _peer_experts(grp, h, ids, gates, x, mod[:, 5], uv, 32)


def _pad_cols(w, width):
    return jnp.pad(w, ((0, 0), (0, width - w.shape[1])))


def _pad_rows(w, height):
    return jnp.pad(w, ((0, height - w.shape[0]), (0, 0)))


def kernel(x_prompt, x_sample, cache_k, cache_v, cache_logf, state_wkv, state_shift, page_table, c_prompt, c_sample, ada_w, ada_b, norm_mix, norm_ffn, att_w_in, att_w_o, att_q_gain, att_k_gain, att_f_bias, rw_mix, rw_w0, rw_w1, rw_w2, rw_a0, rw_a1, rw_a2, rw_g1, rw_g2, rw_k_k, rw_k_a, rw_r_k, rw_w_r, rw_w_k, rw_w_v, rw_w_o, rw_ln_w, rw_ln_b, peer_wq, peer_k1, peer_k2, peer_u, peer_v):
    bp, tp, d = x_prompt.shape
    bs = x_sample.shape[0]
    depth = ada_w.shape[0]
    d_att = H_ATT * HEAD_DIM
    groups = (_Group(bp, tp, 512), _Group(bs, 1, 128))
    xs = [x_prompt.reshape(bp * tp, d), x_sample.reshape(bs, d)]

    c_all = jnp.concatenate([c_prompt, c_sample], axis=0)
    n_seq = c_all.shape[0]
    c_all = jnp.pad(c_all, ((0, (-n_seq) % SUBLANES), (0, 0)))

    outs = {name: [[], []] for name in ("k", "v", "lf", "wkv", "shift")}
    for l in range(depth):
        mod_all = _ada_mod(c_all, ada_w[l], ada_b[l])[:n_seq].reshape(n_seq, N_MOD, d)
        mods = (mod_all[:bp], mod_all[bp:])
        i = l // 2
        if l % 2 == 0:
            w_qkv = att_w_in[i][:, :3 * d_att].astype(BF16)
            w_f = _pad_cols(att_w_in[i][:, 3 * d_att:], LANES).astype(BF16)
            f_bias = _pad_cols(att_f_bias[i][None, :], LANES)
            w_o = att_w_o[i].astype(BF16)
            for gi, grp in enumerate(groups):
                x, mod = xs[gi], mods[gi]
                h = _normmod(grp, x, norm_mix[l], mod[:, 0], mod[:, 1])
                q, k, v, logf = _att_project(grp, h, w_qkv, w_f, att_q_gain[i], att_k_gain[i], f_bias)
                if grp.seq > 1:
                    fcum = _cumsum_seq(logf.reshape(grp.batch, grp.seq, LANES))
                    fcum_t = jnp.swapaxes(fcum[:, :, :H_FOX], 1, 2)
                    o = _attn_prompt(q, k, v, fcum, fcum_t, grp.batch, grp.seq)
                else:
                    pool, page = cache_k.shape[1], cache_k.shape[2]
                    o = _attn_sample(q, k, v, logf,
                                     cache_k[i].reshape(pool, page, d_att), cache_v[i].reshape(pool, page, d_att),
                                     jnp.swapaxes(cache_logf[i], 1, 2), page_table)
                xs[gi] = _proj_residual(grp, o, w_o, x, mod[:, 2], "att_out")
                outs["k"][gi].append(k.reshape(grp.batch, grp.seq, H_ATT, HEAD_DIM))
                outs["v"][gi].append(v.reshape(grp.batch, grp.seq, H_ATT, HEAD_DIM))
                outs["lf"][gi].append(logf[:, :H_FOX].reshape(grp.batch, grp.seq, H_FOX))
        else:
            r_decay, r_aaa = rw_w1.shape[2], rw_a1.shape[2]
            w_rkv = jnp.concatenate([rw_w_r[i], rw_w_k[i], rw_w_v[i]], axis=1).astype(BF16)
            w_lr1 = jnp.concatenate([_pad_cols(rw_w1[i], LANES), _pad_cols(rw_a1[i], LANES), rw_g1[i]],
                                    axis=1).astype(BF16)
            n1 = w_lr1.shape[1]
            w_lr2 = jnp.zeros((n1, 3 * d), F32)
            w_lr2 = w_lr2.at[:r_decay, :d].set(rw_w2[i])
            w_lr2 = w_lr2.at[LANES:LANES + r_aaa, d:2 * d].set(rw_a2[i])
            w_lr2 = w_lr2.at[2 * LANES:, 2 * d:].set(rw_g2[i]).astype(BF16)
            bias_lr2 = jnp.concatenate([rw_w0[i], rw_a0[i], jnp.zeros((d,), F32)])[None, :]
            params = _pad_rows(jnp.stack([rw_k_k[i], rw_k_a[i], rw_r_k[i].reshape(d), rw_ln_w[i], rw_ln_b[i]]),
                               SUBLANES)
            w_o = rw_w_o[i].astype(BF16)
            for gi, grp in enumerate(groups):
                x, mod = xs[gi], mods[gi]
                h = _normmod(grp, x, norm_mix[l], mod[:, 0], mod[:, 1])
                h3 = h.reshape(grp.batch, grp.seq, d)
                if grp.seq > 1:
                    prev = jnp.concatenate([jnp.zeros((grp.batch, 1, d), F32), h3[:, :-1]], axis=1)
                else:
                    prev = state_shift[i][:, None, :]
                rkv, lag = _rwkv_project(grp, h, prev.reshape(grp.m, d), rw_mix[i], w_rkv, w_lr1, w_lr2, bias_lr2)
                if grp.seq > 1:
                    state0 = jnp.zeros((grp.batch, d // RWKV_HS, RWKV_HS, RWKV_HS), F32)
                    seq3 = lambda a: a.reshape(grp.batch, grp.seq, 3 * d)
                    z, s_fin = _rwkv_scan(seq3(rkv), seq3(lag), params, state0, grp.batch, grp.seq,
                                          min(64, grp.seq), 1)
                    z = z.reshape(grp.m, d)
                else:
                    z, s_fin = _rwkv_step(rkv, lag, params, state_wkv[i])
                xs[gi] = _proj_residual(grp, z, w_o, x, mod[:, 2], "rwkv_out")
                outs["wkv"][gi].append(s_fin)
                outs["shift"][gi].append(h3[:, -1])
        n_exp = peer_u.shape[1]
        slab = lambda w: w.reshape(n_exp, d // LANES, LANES)
        uv = jnp.concatenate([slab(peer_u[l]), slab(peer_v[l])], axis=1)
        w_q = peer_wq[l].astype(BF16)
        for gi, grp in enumerate(groups):
            xs[gi] = _peer(grp, xs[gi], mods[gi], norm_ffn[l], w_q, peer_k1[l], peer_k2[l], uv)

    st = lambda name, gi: jnp.stack(outs[name][gi])
    return (xs[0].reshape(bp, tp, d), xs[1].reshape(bs, 1, d),
            st("k", 0), st("v", 0), st("lf", 0), st("wkv", 0), st("shift", 0),
            st("k", 1), st("v", 1), st("lf", 1), st("wkv", 1), st("shift", 1))
```

```python
import functools

import jax
import jax.numpy as jnp
from jax import lax
from jax.experimental import pallas as pl
from jax.experimental.pallas import tpu as pltpu

F32 = jnp.float32
BF16 = jnp.bfloat16

HEAD_DIM = 128
H_FOX = 8
H_ATT = 16
ATT_SCALE = HEAD_DIM ** -0.5
NEG_INF = -1e30
RWKV_HS = 64
LNX_EPS = 64e-5
PEER_HEADS = 8
PEER_TOPK = 16
N_KEYS = 128
N_MOD = 6
RMS_EPS = 1e-6

LANES = 128
SUBLANES = 8
VMEM_LIMIT_BYTES = 56 * 1024 * 1024


def _cparams(sem):
    return pltpu.CompilerParams(dimension_semantics=sem, vmem_limit_bytes=VMEM_LIMIT_BYTES)


def _split_bf16(x, parts):
    out = []
    for _ in range(parts - 1):
        hi = x.astype(BF16)
        out.append(hi)
        x = x - hi.astype(F32)
    out.append(x.astype(BF16))
    return out


def _dot_exact_rhs(x, rhs_bf16, parts=3):
    acc = None
    for p in _split_bf16(x, parts):
        t = jnp.dot(p, rhs_bf16, preferred_element_type=F32)
        acc = t if acc is None else acc + t
    return acc


def _dot_exact_lhs(lhs_bf16, x, parts=3):
    acc = None
    for p in _split_bf16(x, parts):
        t = jnp.dot(lhs_bf16, p, preferred_element_type=F32)
        acc = t if acc is None else acc + t
    return acc


def _dotf(a, b):
    a3 = _split_bf16(a, 3)
    b3 = _split_bf16(b, 3)
    acc = None
    for i, j in ((2, 0), (0, 2), (1, 1), (1, 0), (0, 1), (0, 0)):
        t = jnp.dot(a3[i], b3[j], preferred_element_type=F32)
        acc = t if acc is None else acc + t
    return acc


def _dotf_nt(a, b):
    a3 = _split_bf16(a, 3)
    b3 = _split_bf16(b, 3)
    acc = None
    dn = (((1,), (1,)), ((), ()))
    for i, j in ((2, 0), (0, 2), (1, 1), (1, 0), (0, 1), (0, 0)):
        t = lax.dot_general(a3[i], b3[j], dn, preferred_element_type=F32)
        acc = t if acc is None else acc + t
    return acc


def _softplus(z):
    return jnp.maximum(z, 0.0) + jnp.log1p(jnp.exp(-jnp.abs(z)))


def _sigmoid(z):
    return 1.0 / (1.0 + jnp.exp(-z))


class _Group:
    def __init__(self, batch, seq, tm):
        self.batch, self.seq = batch, seq
        self.m = batch * seq
        if seq == 1:
            self.tm = min(tm, self.m)
        else:
            self.tm = min(tm, seq)
        assert self.m % self.tm == 0 and (seq == 1 or seq % self.tm == 0)
        self.tiles_per_seq = max(seq // self.tm, 1)

    def seq_vec(self, v):
        return v if self.seq == 1 else v[:, None, :]

    def seq_spec(self, width, col=None):
        col = col or (lambda *ids: 0)
        if self.seq == 1:
            return pl.BlockSpec((self.tm, width), lambda *ids: (ids[0], col(*ids)))
        tps = self.tiles_per_seq
        return pl.BlockSpec((None, 1, width), lambda *ids: (ids[0] // tps, 0, col(*ids)))


def _normmod_body(x_ref, g_ref, shift_ref, scale_ref, o_ref):
    x = x_ref[...]
    h = x * lax.rsqrt(jnp.mean(x * x, axis=-1, keepdims=True) + RMS_EPS) * g_ref[...]
    o_ref[...] = h * (1.0 + scale_ref[...]) + shift_ref[...]


def _normmod(grp, x, g, shift, scale):
    m, d = x.shape
    return pl.pallas_call(
        _normmod_body,
        grid=(m // grp.tm,),
        in_specs=[pl.BlockSpec((grp.tm, d), lambda i: (i, 0)),
                  pl.BlockSpec((1, d), lambda i: (0, 0)),
                  grp.seq_spec(d), grp.seq_spec(d)],
        out_specs=pl.BlockSpec((grp.tm, d), lambda i: (i, 0)),
        out_shape=jax.ShapeDtypeStruct((m, d), F32),
        compiler_params=_cparams(("parallel",)),
        name="normmod",
    )(x, g[None, :], grp.seq_vec(shift), grp.seq_vec(scale))


def _mm_body(*refs, n_a, n_e, n_out, prologue, epilogue, every_j):
    a_refs = refs[:n_a]
    w_ref = refs[n_a]
    e_refs = refs[n_a + 1:n_a + 1 + n_e]
    o_refs = refs[n_a + 1 + n_e:n_a + 1 + n_e + n_out]
    a_scr = refs[-1]
    j = pl.program_id(1)

    def fill():
        a_scr[...] = prologue(*[r[...] for r in a_refs]).astype(BF16)

    if every_j:
        fill()
    else:
        pl.when(j == 0)(fill)
    acc = jnp.dot(a_scr[...], w_ref[...].astype(BF16), preferred_element_type=F32)
    epilogue(acc, j, e_refs, o_refs)


def _mm(m, tm, w, tn, a_ins, a_specs, e_ins, e_specs, out_shapes, out_specs, prologue, epilogue, name,
        every_j=False):
    k, n = w.shape
    assert m % tm == 0 and n % tn == 0
    body = functools.partial(_mm_body, n_a=len(a_ins), n_e=len(e_ins), n_out=len(out_shapes),
                             prologue=prologue, epilogue=epilogue, every_j=every_j)
    return pl.pallas_call(
        body,
        grid=(m // tm, n // tn),
        in_specs=list(a_specs) + [pl.BlockSpec((k, tn), lambda i, j: (0, j))] + list(e_specs),
        out_specs=list(out_specs),
        out_shape=list(out_shapes),
        scratch_shapes=[pltpu.VMEM((tm, k), BF16)],
        compiler_params=_cparams(("parallel", "arbitrary")),
        name=name,
    )(*a_ins, w, *e_ins)


def _tile_spec(tm, width):
    return pl.BlockSpec((tm, width), lambda i, j: (i, 0))


def _out_spec(tm, tn):
    return pl.BlockSpec((tm, tn), lambda i, j: (i, j))


def _row_spec(tn):
    return pl.BlockSpec((1, tn), lambda i, j: (0, j))


def _ident(a):
    return a


def _epi(fn):
    def run(acc, j, e_refs, o_refs):
        o_refs[0][...] = fn(acc, j, *[r[...] for r in e_refs])
    return run


def _ada_mod(c, w, b):
    m, d = c.shape
    n = w.shape[1]
    tn = 512
    (out,) = _mm(m, m, w, tn, [c], [_tile_spec(m, d)], [b[None, :]], [_row_spec(tn)],
                 [jax.ShapeDtypeStruct((m, n), F32)], [_out_spec(m, tn)],
                 lambda cc: cc * _sigmoid(cc), _epi(lambda acc, j, bb: acc + bb), "ada_mod")
    return out


def _head_rmsnorm(y, gain):
    blocks = []
    for hh in range(y.shape[1] // HEAD_DIM):
        blk = y[:, hh * HEAD_DIM:(hh + 1) * HEAD_DIM]
        ms = jnp.mean(blk * blk, axis=-1, keepdims=True)
        blocks.append(blk * lax.rsqrt(ms + RMS_EPS) * gain)
    return jnp.concatenate(blocks, axis=1)


def _qkv_epilogue(acc, j, e_refs, o_refs):
    gains = e_refs[0]
    q_ref, k_ref, v_ref = o_refs

    @pl.when(j == 0)
    def _():
        q_ref[...] = _head_rmsnorm(acc, gains[0:1, :])

    @pl.when(j == 1)
    def _():
        q_ref[...] = acc

    @pl.when(j == 2)
    def _():
        k_ref[...] = _head_rmsnorm(acc, gains[1:2, :])

    @pl.when(j == 3)
    def _():
        k_ref[...] = acc

    @pl.when(j >= 4)
    def _():
        v_ref[...] = acc


def _att_project(grp, h, w_qkv, w_f, q_gain, k_gain, f_bias):
    m, d = h.shape
    tm = grp.tm
    d_att = H_ATT * HEAD_DIM
    tn = d_att // 2
    gains = jnp.stack([q_gain, k_gain])
    shp = jax.ShapeDtypeStruct((m, d_att), F32)

    def ospec(first):
        return pl.BlockSpec((tm, tn), lambda i, j: (i, jnp.clip(j - first, 0, 1)))

    q, k, v = _mm(m, tm, w_qkv, tn, [h], [_tile_spec(tm, d)],
                  [gains], [pl.BlockSpec((2, HEAD_DIM), lambda i, j: (0, 0))],
                  [shp, shp, shp], [ospec(0), ospec(2), ospec(4)],
                  _ident, _qkv_epilogue, "att_qkv")
    (logf,) = _mm(m, tm, w_f, LANES, [h], [_tile_spec(tm, d)],
                  [f_bias], [_row_spec(LANES)],
                  [jax.ShapeDtypeStruct((m, LANES), F32)], [_out_spec(tm, LANES)],
                  _ident, _epi(lambda acc, j, fb: -_softplus(-(acc + fb))), "att_logf")
    return q, k, v, logf


def _cumsum_body(x_ref, o_ref):
    t = x_ref.shape[0]
    row = lax.broadcasted_iota(jnp.int32, (t, t), 0)
    col = lax.broadcasted_iota(jnp.int32, (t, t), 1)
    tri = (col <= row).astype(BF16)
    o_ref[...] = _dot_exact_lhs(tri, x_ref[...])


def _cumsum_seq(x):
    b, t, w = x.shape
    return pl.pallas_call(
        _cumsum_body,
        grid=(b,),
        in_specs=[pl.BlockSpec((None, t, w), lambda i: (i, 0, 0))],
        out_specs=pl.BlockSpec((None, t, w), lambda i: (i, 0, 0)),
        out_shape=jax.ShapeDtypeStruct((b, t, w), F32),
        compiler_params=_cparams(("parallel",)),
        name="fox_cumsum",
    )(x)


def _attn_prompt_body(q_ref, k_ref, v_ref, fc_ref, ft_ref, o_ref, *, tq):
    h = pl.program_id(1)
    qi = pl.program_id(2)
    q = q_ref[...].astype(BF16)
    t_pos = qi * tq + lax.broadcasted_iota(jnp.int32, (tq, tq), 0)
    s_off = lax.broadcasted_iota(jnp.int32, (tq, tq), 1)
    nt = (((1,), (1,)), ((), ()))

    def scores(kb):
        start = pl.multiple_of(kb * tq, tq)
        kblk = k_ref[pl.ds(start, tq), :].astype(BF16)
        vblk = v_ref[pl.ds(start, tq), :].astype(BF16)
        s = lax.dot_general(q, kblk, nt, preferred_element_type=F32) * ATT_SCALE
        return s, vblk, start

    @pl.when(h < H_FOX)
    def _fox():
        lane = lax.broadcasted_iota(jnp.int32, fc_ref.shape, 1)
        fq = jnp.sum(jnp.where(lane == h, fc_ref[...], 0.0), axis=-1, keepdims=True)

        def step(kb, carry):
            m_run, l_run, acc = carry
            s, vblk, start = scores(kb)
            fk = ft_ref[pl.ds(h, 1), pl.ds(start, tq)]
            s = s + (fq - fk)
            s = jnp.where(kb * tq + s_off <= t_pos, s, NEG_INF)
            m_new = jnp.maximum(m_run, jnp.max(s, axis=-1, keepdims=True))
            alpha = jnp.exp(m_run - m_new)
            p = jnp.exp(s - m_new)
            l_new = alpha * l_run + jnp.sum(p, axis=-1, keepdims=True)
            acc = alpha * acc + jnp.dot(p.astype(BF16), vblk, preferred_element_type=F32)
            return m_new, l_new, acc

        init = (jnp.full((tq, 1), NEG_INF, F32), jnp.zeros((tq, 1), F32), jnp.zeros((tq, HEAD_DIM), F32))
        _, l_fin, acc = lax.fori_loop(0, qi + 1, step, init)
        o_ref[...] = acc / l_fin

    @pl.when(h >= H_FOX)
    def _sb():
        jj = lax.broadcasted_iota(jnp.int32, (tq, tq), 0)
        ss = lax.broadcasted_iota(jnp.int32, (tq, tq), 1)
        upper = (jj > ss).astype(BF16)

        def step(it, carry):
            later, acc = carry
            kb = qi - it
            z, vblk, _ = scores(kb)
            strict = kb * tq + s_off < t_pos
            lk = jnp.where(strict, -_softplus(z), 0.0)
            later_in = _dot_exact_rhs(lk, upper, parts=2)
            a = jnp.where(strict, jnp.exp(z + lk + later_in + later), 0.0)
            acc = acc + jnp.dot(a.astype(BF16), vblk, preferred_element_type=F32)
            later = later + jnp.sum(lk, axis=-1, keepdims=True)
            return later, acc

        init = (jnp.zeros((tq, 1), F32), jnp.zeros((tq, HEAD_DIM), F32))
        _, acc = lax.fori_loop(0, qi + 1, step, init)
        o_ref[...] = acc


def _attn_prompt(q, k, v, fcum, fcum_t, batch, seq):
    d_att = q.shape[1]
    tq = min(256, seq)
    q3, k3, v3 = (a.reshape(batch, seq, d_att) for a in (q, k, v))
    body = functools.partial(_attn_prompt_body, tq=tq)
    out = pl.pallas_call(
        body,
        grid=(batch, H_ATT, seq // tq),
        in_specs=[pl.BlockSpec((None, tq, HEAD_DIM), lambda b, h, i: (b, i, h)),
                  pl.BlockSpec((None, seq, HEAD_DIM), lambda b, h, i: (b, 0, h)),
                  pl.BlockSpec((None, seq, HEAD_DIM), lambda b, h, i: (b, 0, h)),
                  pl.BlockSpec((None, tq, LANES), lambda b, h, i: (b, i, 0)),
                  pl.BlockSpec((None, H_FOX, seq), lambda b, h, i: (b, 0, 0))],
        out_specs=pl.BlockSpec((None, tq, HEAD_DIM), lambda b, h, i: (b, i, h)),
        out_shape=jax.ShapeDtypeStruct((batch, seq, d_att), F32),
        compiler_params=_cparams(("parallel", "parallel", "arbitrary")),
        name="attn_prompt",
    )(q3, k3, v3, fcum, fcum_t)
    return out.reshape(batch * seq, d_att)


def _attn_sample_body(pt_ref, q_ref, kn_ref, vn_ref, lfn_ref, kc_ref, vc_ref, lfc_ref, o_ref,
                      qrows, m_scr, l_scr, suf_scr, acc_scr, *, n_pages):
    p = pl.program_id(1)
    d_att = q_ref.shape[-1]
    page = kc_ref.shape[0]
    sub = lax.broadcasted_iota(jnp.int32, (H_ATT, d_att), 0)
    lane_head = lax.broadcasted_iota(jnp.int32, (H_ATT, d_att), 1) // HEAD_DIM
    own = sub == lane_head
    is_fox = lax.broadcasted_iota(jnp.int32, (H_ATT, 1), 0) < H_FOX
    nt = (((1,), (1,)), ((), ()))

    @pl.when(p == 0)
    def _init():
        qr = jnp.where(own, q_ref[...], 0.0)
        qrows[...] = qr.astype(BF16)
        s_self = jnp.sum(qr * kn_ref[...], axis=-1, keepdims=True) * ATT_SCALE
        m_scr[...] = jnp.where(is_fox, s_self, 0.0)
        l_scr[...] = jnp.ones_like(l_scr)
        acc_scr[...] = jnp.where(sub < H_FOX, jnp.broadcast_to(vn_ref[...], (H_ATT, d_att)), 0.0)
        lf_lane = lax.broadcasted_iota(jnp.int32, (H_ATT, LANES), 1)
        lf_sub = lax.broadcasted_iota(jnp.int32, (H_ATT, LANES), 0)
        lf_new = jnp.sum(jnp.where(lf_lane == lf_sub, lfn_ref[...], 0.0), axis=-1, keepdims=True)
        suf_scr[...] = jnp.where(is_fox, lf_new, 0.0)

    s = lax.dot_general(qrows[...], kc_ref[...].astype(BF16), nt,
                        preferred_element_type=F32) * ATT_SCALE
    jj = lax.broadcasted_iota(jnp.int32, (page, page), 0)
    ss = lax.broadcasted_iota(jnp.int32, (page, page), 1)
    upper = (jj > ss).astype(BF16)
    lk = -_softplus(s)
    lf = jnp.concatenate([lfc_ref[...], jnp.zeros((H_ATT - H_FOX, page), F32)], axis=0)
    step_log = jnp.where(is_fox, lf, lk)
    later = _dot_exact_rhs(step_log, upper) + suf_scr[...]
    logit = s + later + jnp.where(is_fox, 0.0, lk)
    m_old = m_scr[...]
    m_new = jnp.where(is_fox, jnp.maximum(m_old, jnp.max(logit, axis=-1, keepdims=True)), 0.0)
    alpha = jnp.exp(m_old - m_new)
    w = jnp.exp(logit - m_new)
    l_scr[...] = jnp.where(is_fox, alpha * l_scr[...] + jnp.sum(w, axis=-1, keepdims=True), 1.0)
    acc_scr[...] = alpha * acc_scr[...] + jnp.dot(w.astype(BF16), vc_ref[...].astype(BF16),
                                                  preferred_element_type=F32)
    m_scr[...] = m_new
    suf_scr[...] = suf_scr[...] + jnp.sum(step_log, axis=-1, keepdims=True)

    @pl.when(p == n_pages - 1)
    def _fin():
        full = acc_scr[...] / l_scr[...]
        o_ref[...] = jnp.sum(jnp.where(own, full, 0.0), axis=0, keepdims=True)


def _attn_sample(q, k_new, v_new, logf_new, cache_k, cache_v, cache_lf_t, page_table):
    b, d_att = q.shape
    n_pages = page_table.shape[1]
    page = cache_k.shape[1]
    row = lambda a: a[:, None, :]

    def new_spec(width):
        return pl.BlockSpec((None, 1, width), lambda i, p, pt: (i, 0, 0))

    def page_spec(shape):
        return pl.BlockSpec((None,) + shape, lambda i, p, pt: (pt[i, n_pages - 1 - p], 0, 0))

    grid_spec = pltpu.PrefetchScalarGridSpec(
        num_scalar_prefetch=1,
        grid=(b, n_pages),
        in_specs=[new_spec(d_att), new_spec(d_att), new_spec(d_att), new_spec(LANES),
                  page_spec((page, d_att)), page_spec((page, d_att)), page_spec((H_FOX, page))],
        out_specs=pl.BlockSpec((None, 1, d_att), lambda i, p, pt: (i, 0, 0)),
        scratch_shapes=[pltpu.VMEM((H_ATT, d_att), BF16), pltpu.VMEM((H_ATT, 1), F32),
                        pltpu.VMEM((H_ATT, 1), F32), pltpu.VMEM((H_ATT, 1), F32),
                        pltpu.VMEM((H_ATT, d_att), F32)],
    )
    out = pl.pallas_call(
        functools.partial(_attn_sample_body, n_pages=n_pages),
        grid_spec=grid_spec,
        out_shape=jax.ShapeDtypeStruct((b, 1, d_att), F32),
        compiler_params=_cparams(("parallel", "arbitrary")),
        name="attn_sample",
    )(page_table, row(q), row(k_new), row(v_new), row(logf_new), cache_k, cache_v, cache_lf_t)
    return out.reshape(b, d_att)


def _proj_residual(grp, a, w, x, gate, name):
    m, d = x.shape
    tm, tn = grp.tm, min(1024, d)
    (out,) = _mm(m, tm, w, tn, [a], [_tile_spec(tm, a.shape[1])],
                 [x, grp.seq_vec(gate)], [_out_spec(tm, tn), grp.seq_spec(tn, col=lambda i, j: j)],
                 [jax.ShapeDtypeStruct((m, d), F32)], [_out_spec(tm, tn)],
                 _ident, _epi(lambda acc, j, xx, gg: xx + gg * acc), name)
    return out


def _mix_prologue(h, prev, mix):
    return h + (prev - h) * mix


def _rwkv_project(grp, h, prev, mix, w_rkv, w_lr1, w_lr2, bias_lr2):
    m, d = h.shape
    tm = grp.tm
    tn = d // 2
    mix_rkv = jnp.stack([mix[0], mix[2], mix[3]])[:, None, :]
    (rkv,) = _mm(m, tm, w_rkv, tn, [h, prev, mix_rkv],
                 [_tile_spec(tm, d), _tile_spec(tm, d),
                  pl.BlockSpec((None, 1, d), lambda i, j: (j // 2, 0, 0))],
                 [], [], [jax.ShapeDtypeStruct((m, 3 * d), F32)], [_out_spec(tm, tn)],
                 _mix_prologue, _epi(lambda acc, j: acc), "rwkv_rkv", every_j=True)

    mix_lr = jnp.stack([mix[1], mix[4], mix[5], mix[5]])[:, None, :]

    def lr1_act(acc, j):
        return lax.cond(j == 0, jnp.tanh, lambda a: lax.cond(j == 1, _ident, _sigmoid, a), acc)

    n1 = w_lr1.shape[1]
    (lr1,) = _mm(m, tm, w_lr1, LANES, [h, prev, mix_lr],
                 [_tile_spec(tm, d), _tile_spec(tm, d),
                  pl.BlockSpec((None, 1, d), lambda i, j: (j, 0, 0))],
                 [], [], [jax.ShapeDtypeStruct((m, n1), F32)], [_out_spec(tm, LANES)],
                 _mix_prologue, _epi(lr1_act), "rwkv_lr1", every_j=True)

    def lr2_act(acc, j, bias):
        y = acc + bias
        decay = lambda a: -jnp.exp(-_softplus(-a) - 0.5)
        return lax.cond(j < 2, decay, lambda a: lax.cond(j < 4, _sigmoid, _ident, a), y)

    (lag,) = _mm(m, tm, w_lr2, tn, [lr1], [_tile_spec(tm, n1)],
                 [bias_lr2], [_row_spec(tn)],
                 [jax.ShapeDtypeStruct((m, 3 * d), F32)], [_out_spec(tm, tn)],
                 _ident, _epi(lr2_act), "rwkv_lr2")
    return rkv, lag


def _dot3(a, b, dn):
    return lax.dot_general(a.astype(BF16), b.astype(BF16), dn, preferred_element_type=F32)


_NN = (((1,), (0,)), ((), ()))
_NT = (((1,), (1,)), ((), ()))


def _half_sum(x, lo_half):
    s0 = jnp.sum(jnp.where(lo_half, x, 0.0), axis=-1, keepdims=True)
    s1 = jnp.sum(jnp.where(lo_half, 0.0, x), axis=-1, keepdims=True)
    return jnp.where(lo_half, s0, s1)


def _rwkv_scan_body(r_ref, k_ref, v_ref, lw_ref, a_ref, g_ref, par_ref, s0_ref, z_ref, sT_ref,
                    s_scr, a2_scr, u0_scr, o0_scr, rh_scr, bp_scr, no_scr, kv_scr, last_scr,
                    *, chunk, n_chunks, n_pairs):
    hs = RWKV_HS
    blk = 2 * hs
    lane = lax.broadcasted_iota(jnp.int32, (1, blk), 1)
    lo_half = lane < hs
    hi_half = jnp.logical_not(lo_half)
    row = lax.broadcasted_iota(jnp.int32, (chunk, chunk), 0)
    col = lax.broadcasted_iota(jnp.int32, (chunk, chunk), 1)
    tril_incl = (col <= row)
    tril_strict = (col < row)
    eye = (col == row).astype(F32)
    rr = lax.broadcasted_iota(jnp.int32, (blk, blk), 0) < hs
    cc = lax.broadcasted_iota(jnp.int32, (blk, blk), 1) < hs
    same_head = rr == cc
    n_double = max(chunk.bit_length() - 2, 0)
    zero = jnp.zeros((hs, hs), F32)

    def lanes(pi):
        return slice(pi * blk, (pi + 1) * blk)

    def params(pi):
        return [par_ref[i:i + 1, lanes(pi)] for i in range(5)]

    def prepare(c, pi):
        sl = pl.ds(pl.multiple_of(c * chunk, chunk), chunk)
        r, k, v, lw, a = (ref[sl, lanes(pi)] for ref in (r_ref, k_ref, v_ref, lw_ref, a_ref))
        k_k, k_a = params(pi)[:2]
        kk = k * k_k
        kk = kk / jnp.maximum(jnp.sqrt(_half_sum(kk * kk, lo_half)), 1e-12)
        kmod = k * (1.0 + (a - 1.0) * k_a)
        cum = _dot_exact_lhs(tril_incl.astype(BF16), lw)
        dec_in = jnp.exp(cum)
        inv = jnp.exp(-cum)
        a_hat, b_hat, k_hat, r_hat = -kk * jnp.exp(cum - lw), kk * a * inv, kmod * inv, r * dec_in
        last = dec_in[chunk - 1:chunk, :]

        def solve(msk):
            am = jnp.where(msk, a_hat, 0.0)
            n_mat = jnp.where(tril_strict, _dot3(am, b_hat, _NT), 0.0)
            m_mat = jnp.where(tril_strict, _dot3(am, k_hat, _NT), 0.0)
            inv_mat, pw = eye + n_mat, n_mat
            for _ in range(n_double):
                pw = _dot3(pw, pw, _NN)
                inv_mat = inv_mat + _dot3(inv_mat, pw, _NN)
            return _dot3(inv_mat, a_hat, _NN), _dot3(inv_mat, _dot3(m_mat, v, _NN), _NN)

        a2_lo, u0_lo = solve(lo_half)
        a2_hi, u0_hi = solve(hi_half)
        a2_scr[pi, c] = jnp.where(lo_half, a2_lo, a2_hi)
        u0_scr[pi, c] = jnp.where(lo_half, u0_lo, u0_hi)

        def readout(msk):
            rm = jnp.where(msk, r_hat, 0.0)
            no = jnp.where(tril_incl, _dot3(rm, b_hat, _NT), 0.0)
            mo = jnp.where(tril_incl, _dot3(rm, k_hat, _NT), 0.0)
            return no, _dot3(mo, v, _NN)

        no_lo, o0_lo = readout(lo_half)
        no_hi, o0_hi = readout(hi_half)
        no_scr[pi, c, 0] = no_lo
        no_scr[pi, c, 1] = no_hi
        o0_scr[pi, c] = jnp.where(lo_half, o0_lo, o0_hi)
        rh_scr[pi, c] = r_hat
        bp_scr[pi, c] = b_hat * last
        kv_scr[pi, c] = jnp.where(same_head, _dot3(v.T, k_hat * last, _NN), 0.0)
        last_scr[pi, c] = jnp.broadcast_to(last, (SUBLANES, blk))

    def phase1(c, _):
        for pi in range(n_pairs):
            prepare(c, pi)
        return 0

    lax.fori_loop(0, n_chunks, phase1, 0, unroll=2 if n_chunks % 2 == 0 else 1)

    for pi in range(n_pairs):
        s_scr[pi] = jnp.concatenate([jnp.concatenate([s0_ref[2 * pi], zero], axis=1),
                                     jnp.concatenate([zero, s0_ref[2 * pi + 1]], axis=1)], axis=0)

    def advance(c, pi):
        sl = pl.ds(pl.multiple_of(c * chunk, chunk), chunk)
        state = s_scr[pi]
        last = last_scr[pi, c][0:1, :]
        u = u0_scr[pi, c] + _dot3(a2_scr[pi, c], state, _NT)
        s_scr[pi] = jnp.where(same_head, state * last + _dot3(u.T, bp_scr[pi, c], _NN), 0.0) + kv_scr[pi, c]
        o = (_dot3(rh_scr[pi, c], state, _NT) + o0_scr[pi, c]
             + jnp.where(lo_half, _dot3(no_scr[pi, c, 0], u, _NN), _dot3(no_scr[pi, c, 1], u, _NN)))

        r, k, v, a, g = (ref[sl, lanes(pi)] for ref in (r_ref, k_ref, v_ref, a_ref, g_ref))
        _, k_a, r_k, ln_w, ln_b = params(pi)
        kmod = k * (1.0 + (a - 1.0) * k_a)
        mu = _half_sum(o, lo_half) / hs
        dev = o - mu
        var = _half_sum(dev * dev, lo_half) / hs
        o_n = dev * lax.rsqrt(var + LNX_EPS) * ln_w + ln_b
        bonus = _half_sum(r * kmod * r_k, lo_half) * v
        z_ref[sl, lanes(pi)] = (o_n + bonus) * g

    def phase2(c, _):
        for pi in range(n_pairs):
            advance(c, pi)
        return 0

    lax.fori_loop(0, n_chunks, phase2, 0)
    for pi in range(n_pairs):
        fin = s_scr[pi]
        sT_ref[2 * pi] = fin[:hs, :hs]
        sT_ref[2 * pi + 1] = fin[hs:, hs:]


def _rwkv_scan(rkv, lag, params, state0, batch, seq, chunk, n_pairs):
    d = rkv.shape[-1] // 3
    blk = 2 * RWKV_HS
    width = n_pairs * blk
    groups = d // width
    n_chunks = seq // chunk

    def seq_spec(part):
        return pl.BlockSpec((None, seq, width), lambda b, p: (b, 0, part * groups + p))

    st_spec = pl.BlockSpec((None, 2 * n_pairs, RWKV_HS, RWKV_HS), lambda b, p: (b, p, 0, 0))
    body = functools.partial(_rwkv_scan_body, chunk=chunk, n_chunks=n_chunks, n_pairs=n_pairs)
    per_chunk = lambda *shape: pltpu.VMEM((n_pairs, n_chunks) + shape, F32)
    z, s_fin = pl.pallas_call(
        body,
        grid=(batch, groups),
        in_specs=[seq_spec(0), seq_spec(1), seq_spec(2), seq_spec(0), seq_spec(1), seq_spec(2),
                  pl.BlockSpec((SUBLANES, width), lambda b, p: (0, p)), st_spec],
        out_specs=[pl.BlockSpec((None, seq, width), lambda b, p: (b, 0, p)), st_spec],
        out_shape=[jax.ShapeDtypeStruct((batch, seq, d), F32),
                   jax.ShapeDtypeStruct(state0.shape, F32)],
        scratch_shapes=[pltpu.VMEM((n_pairs, blk, blk), F32),
                        per_chunk(chunk, blk), per_chunk(chunk, blk), per_chunk(chunk, blk),
                        per_chunk(chunk, blk), per_chunk(chunk, blk), per_chunk(2, chunk, chunk),
                        per_chunk(blk, blk), per_chunk(SUBLANES, blk)],
        compiler_params=_cparams(("parallel", "parallel")),
        name="rwkv_scan",
    )(rkv, rkv, rkv, lag, lag, lag, params, state0)
    return z, s_fin


def _rwkv_step_body(rkv_ref, lag_ref, par_ref, s0_ref, z_ref, s1_ref):
    hs = RWKV_HS
    r, k, v = rkv_ref[0], rkv_ref[1], rkv_ref[2]
    lw, a, g = lag_ref[0], lag_ref[1], lag_ref[2]
    k_k, k_a, r_k, ln_w, ln_b = (par_ref[i] for i in range(5))
    eye = (lax.broadcasted_iota(jnp.int32, (hs, hs), 0) == lax.broadcasted_iota(jnp.int32, (hs, hs), 1))
    kk = k * k_k
    kk = kk / jnp.maximum(jnp.sqrt(jnp.sum(kk * kk, axis=-1, keepdims=True)), 1e-12)
    kmod = k * (1.0 + (a - 1.0) * k_a)
    state = s0_ref[...]
    sa = jnp.sum(state * (-kk), axis=-1, keepdims=True)
    v_col = jnp.sum(jnp.where(eye, v, 0.0), axis=-1, keepdims=True)
    new = state * jnp.exp(lw) + sa * (kk * a) + v_col * kmod
    s1_ref[...] = new
    o_col = jnp.sum(new * r, axis=-1, keepdims=True)
    o = jnp.sum(jnp.where(eye, o_col, 0.0), axis=1, keepdims=True)
    mu = jnp.mean(o, axis=-1, keepdims=True)
    dev = o - mu
    var = jnp.mean(dev * dev, axis=-1, keepdims=True)
    o_n = dev * lax.rsqrt(var + LNX_EPS) * ln_w + ln_b
    bonus = jnp.sum(r * kmod * r_k, axis=-1, keepdims=True) * v
    z_ref[...] = (o_n + bonus) * g


def _rwkv_step(rkv, lag, params, state0):
    b, heads, hs, _ = state0.shape
    vec = lambda x: x.reshape(b, 3, heads, 1, hs)
    vec_spec = pl.BlockSpec((None, 3, heads, 1, hs), lambda i: (i, 0, 0, 0, 0))
    st_spec = pl.BlockSpec((None, heads, hs, hs), lambda i: (i, 0, 0, 0))
    z, s_fin = pl.pallas_call(
        _rwkv_step_body,
        grid=(b,),
        in_specs=[vec_spec, vec_spec, pl.BlockSpec((SUBLANES, heads, 1, hs), lambda i: (0, 0, 0, 0)), st_spec],
        out_specs=[pl.BlockSpec((None, heads, 1, hs), lambda i: (i, 0, 0, 0)), st_spec],
        out_shape=[jax.ShapeDtypeStruct((b, heads, 1, hs), F32), jax.ShapeDtypeStruct(state0.shape, F32)],
        compiler_params=_cparams(("parallel",)),
        name="rwkv_step",
    )(vec(rkv), vec(lag), params.reshape(SUBLANES, heads, 1, hs), state0)
    return z.reshape(b, heads * hs), s_fin


def _peer_route_body(q_ref, k1_ref, k2_ref, ids_ref, gate_ref):
    tm = q_ref.shape[0]
    topk = PEER_TOPK
    n_cand = topk * topk
    half = k1_ref.shape[-1]
    key_row = lax.broadcasted_iota(jnp.int32, (N_KEYS, tm), 0).astype(F32)
    cand_row = lax.broadcasted_iota(jnp.int32, (n_cand, tm), 0).astype(F32)
    rank = lax.broadcasted_iota(jnp.int32, (topk, tm), 0)
    rank_f = rank.astype(F32)

    def take_max(s, rows, n_rows):
        m = jnp.max(s, axis=0, keepdims=True)
        idx = jnp.min(jnp.where(s == m, rows, float(n_rows)), axis=0, keepdims=True)
        return m, idx, jnp.where(rows == idx, -jnp.inf, s)

    vals_heads, ids_heads = [], []
    for hq in range(PEER_HEADS):
        q1 = q_ref[:, hq * 2 * half:hq * 2 * half + half].astype(BF16)
        q2 = q_ref[:, hq * 2 * half + half:(hq + 1) * 2 * half].astype(BF16)
        s1 = lax.dot_general(k1_ref[hq].astype(BF16), q1, _NT, preferred_element_type=F32)
        s2 = lax.dot_general(k2_ref[hq].astype(BF16), q2, _NT, preferred_element_type=F32)

        def sub_it(i, carry):
            s1, s2, v1, e1, v2, e2 = carry
            m1, i1, s1 = take_max(s1, key_row, N_KEYS)
            m2, i2, s2 = take_max(s2, key_row, N_KEYS)
            here = rank == i
            return (s1, s2, jnp.where(here, m1, v1), jnp.where(here, i1, e1),
                    jnp.where(here, m2, v2), jnp.where(here, i2, e2))

        zero = jnp.zeros((topk, tm), F32)
        _, _, v1, e1, v2, e2 = lax.fori_loop(0, topk, sub_it, (s1, s2, zero, zero, zero, zero))
        cs = jnp.concatenate([v1[i:i + 1, :] + v2 for i in range(topk)], axis=0)

        def cand_it(i, carry):
            cs, vals, ids = carry
            m, pos, cs = take_max(cs, cand_row, n_cand)
            hi = jnp.floor(pos * (1.0 / topk))
            lo = pos - hi * topk
            k_hi = jnp.sum(jnp.where(rank_f == hi, e1, 0.0), axis=0, keepdims=True)
            k_lo = jnp.sum(jnp.where(rank_f == lo, e2, 0.0), axis=0, keepdims=True)
            here = rank == i
            return cs, jnp.where(here, m, vals), jnp.where(here, k_hi * N_KEYS + k_lo, ids)

        _, vals, ids = lax.fori_loop(0, topk, cand_it, (cs, zero, zero))
        ex = jnp.exp(vals - jnp.max(vals, axis=0, keepdims=True))
        vals_heads.append(ex / jnp.sum(ex, axis=0, keepdims=True))
        ids_heads.append(ids)

    gate_ref[...] = jnp.concatenate(vals_heads, axis=0).T
    ids_ref[...] = jnp.concatenate(ids_heads, axis=0).T.astype(jnp.int32)


def _peer_route(q, k1, k2, tm):
    m, dq = q.shape
    assert PEER_HEADS * PEER_TOPK == N_KEYS
    kspec = pl.BlockSpec(k1.shape, lambda i: (0, 0, 0))
    ospec = pl.BlockSpec((tm, N_KEYS), lambda i: (i, 0))
    return pl.pallas_call(
        _peer_route_body,
        grid=(m // tm,),
        in_specs=[pl.BlockSpec((tm, dq), lambda i: (i, 0)), kspec, kspec],
        out_specs=[ospec, ospec],
        out_shape=[jax.ShapeDtypeStruct((m, N_KEYS), jnp.int32), jax.ShapeDtypeStruct((m, N_KEYS), F32)],
        compiler_params=_cparams(("parallel",)),
        name="peer_route",
    )(q, k1, k2)


def _gelu(x):
    return 0.5 * x * (1.0 + lax.erf(x * (2.0 ** -0.5)))


PEER_BUFFERS = 4


def _peer_expert_body(ids_ref, nxt_ref, h_ref, gate_ref, x_ref, g5_ref, uv_hbm, o_ref, *scratch, tt):
    nb = PEER_BUFFERS
    bufs, (sem, pout) = scratch[:nb], scratch[nb:]
    d = h_ref.shape[1]
    n_sel = gate_ref.shape[1]
    step = pl.program_id(0)
    last_step = pl.num_programs(0) - 1
    diag = (lax.broadcasted_iota(jnp.int32, (n_sel, n_sel), 0)
            == lax.broadcasted_iota(jnp.int32, (n_sel, n_sel), 1))

    n_rows = uv_hbm.shape[1] // 2
    seg = uv_hbm.shape[2]

    def issue(src_ref, row, slot):
        for p in range(n_sel):
            pltpu.make_async_copy(uv_hbm.at[src_ref[row, p]], bufs[slot].at[:, p], sem.at[slot]).start()

    def wait_all(slot):
        pltpu.make_async_copy(bufs[slot], bufs[slot], sem.at[slot]).wait()

    def compute(t, slot):
        part = None
        hrow = h_ref[pl.ds(t, 1), :]
        for r in range(n_rows):
            term = bufs[slot][r] * hrow[:, r * seg:(r + 1) * seg]
            part = term if part is None else part + term
        act = jnp.sum(part, axis=1, keepdims=True)
        gate = jnp.sum(jnp.where(diag, gate_ref[pl.ds(t, 1), :], 0.0), axis=1, keepdims=True)
        wgt = _gelu(act) * gate
        pout[pl.ds(t, 1), :] = jnp.concatenate(
            [jnp.sum(bufs[slot][n_rows + r] * wgt, axis=0, keepdims=True) for r in range(n_rows)], axis=1)

    ahead = nb - 1

    @pl.when(step == 0)
    def _():
        for j in range(ahead):
            issue(ids_ref, j, j)

    def token_group(k, _):
        for j in range(nb):
            t = nb * k + j
            issue(ids_ref, t + ahead, (j + ahead) % nb)
            wait_all(j)
            compute(t, j)
        return 0

    lax.fori_loop(0, tt // nb - 1, token_group, 0)
    for j in range(nb):
        t = tt - nb + j
        if t + ahead < tt:
            issue(ids_ref, t + ahead, (j + ahead) % nb)
        else:
            issue(nxt_ref, t + ahead - tt, (j + ahead) % nb)
        wait_all(j)
        compute(t, j)

    @pl.when(step == last_step)
    def _():
        for j in range(ahead):
            wait_all(j)

    o_ref[...] = x_ref[...] + g5_ref[...] * pout[...]


def _peer_experts(grp, h, ids, gates, x, gate5, uv, tt):
    m, d = h.shape
    n_sel = ids.shape[1]
    tt = min(tt, grp.tm)
    assert grp.tm % tt == 0 and tt % PEER_BUFFERS == 0
    sub = grp.tm // tt
    n_steps = m // tt
    ids_spec = lambda nxt: pl.BlockSpec((tt, n_sel), lambda i: (jnp.minimum(i + nxt, n_steps - 1), 0),
                                        memory_space=pltpu.SMEM)
    tile = lambda width: pl.BlockSpec((tt, width), lambda i: (i, 0))
    if grp.seq == 1:
        g5, g5_spec = gate5, tile(d)
    else:
        tps = grp.tiles_per_seq * sub
        g5, g5_spec = gate5[:, None, :], pl.BlockSpec((None, 1, d), lambda i: (i // tps, 0, 0))
    return pl.pallas_call(
        functools.partial(_peer_expert_body, tt=tt),
        grid=(n_steps,),
        in_specs=[ids_spec(0), ids_spec(1),
                  tile(d), tile(n_sel), tile(d), g5_spec,
                  pl.BlockSpec(memory_space=pl.ANY)],
        out_specs=tile(d),
        out_shape=jax.ShapeDtypeStruct((m, d), F32),
        scratch_shapes=[pltpu.VMEM((uv.shape[1], n_sel, uv.shape[2]), F32)] * PEER_BUFFERS
        + [pltpu.SemaphoreType.DMA((PEER_BUFFERS,)), pltpu.VMEM((tt, d), F32)],
        compiler_params=_cparams(("arbitrary",)),
        name="peer_experts",
    )(ids, ids, h, gates, x, g5, uv)


def _peer(grp, x, mod, norm_g, w_q, k1, k2, uv):
    h = _normmod(grp, x, norm_g, mod[:, 3], mod[:, 4])
    m, d = h.shape
    tn = min(1024, w_q.shape[1])
    (q,) = _mm(m, grp.tm, w_q, tn, [h], [_tile_spec(grp.tm, d)], [], [],
               [jax.ShapeDtypeStruct((m, w_q.shape[1]), F32)], [_out_spec(grp.tm, tn)],
               _ident, _epi(lambda acc, j: acc), "peer_q")
    ids, gates = _peer_route(q, k1, k2, min(128, m))
    return _peer_experts(grp, h, ids, gates, x, mod[:, 5], uv, 32)


def _pad_cols(w, width):
    return jnp.pad(w, ((0, 0), (0, width - w.shape[1])))


def _pad_rows(w, height):
    return jnp.pad(w, ((0, height - w.shape[0]), (0, 0)))


def kernel(x_prompt, x_sample, cache_k, cache_v, cache_logf, state_wkv, state_shift, page_table, c_prompt, c_sample, ada_w, ada_b, norm_mix, norm_ffn, att_w_in, att_w_o, att_q_gain, att_k_gain, att_f_bias, rw_mix, rw_w0, rw_w1, rw_w2, rw_a0, rw_a1, rw_a2, rw_g1, rw_g2, rw_k_k, rw_k_a, rw_r_k, rw_w_r, rw_w_k, rw_w_v, rw_w_o, rw_ln_w, rw_ln_b, peer_wq, peer_k1, peer_k2, peer_u, peer_v):
    bp, tp, d = x_prompt.shape
    bs = x_sample.shape[0]
    depth = ada_w.shape[0]
    d_att = H_ATT * HEAD_DIM
    groups = (_Group(bp, tp, 512), _Group(bs, 1, 128))
    xs = [x_prompt.reshape(bp * tp, d), x_sample.reshape(bs, d)]

    c_all = jnp.concatenate([c_prompt, c_sample], axis=0)
    n_seq = c_all.shape[0]
    c_all = jnp.pad(c_all, ((0, (-n_seq) % SUBLANES), (0, 0)))

    outs = {name: [[], []] for name in ("k", "v", "lf", "wkv", "shift")}
    for l in range(depth):
        mod_all = _ada_mod(c_all, ada_w[l], ada_b[l])[:n_seq].reshape(n_seq, N_MOD, d)
        mods = (mod_all[:bp], mod_all[bp:])
        i = l // 2
        if l % 2 == 0:
            w_qkv = att_w_in[i][:, :3 * d_att].astype(BF16)
            w_f = _pad_cols(att_w_in[i][:, 3 * d_att:], LANES).astype(BF16)
            f_bias = _pad_cols(att_f_bias[i][None, :], LANES)
            w_o = att_w_o[i].astype(BF16)
            for gi, grp in enumerate(groups):
                x, mod = xs[gi], mods[gi]
                h = _normmod(grp, x, norm_mix[l], mod[:, 0], mod[:, 1])
                q, k, v, logf = _att_project(grp, h, w_qkv, w_f, att_q_gain[i], att_k_gain[i], f_bias)
                if grp.seq > 1:
                    fcum = _cumsum_seq(logf.reshape(grp.batch, grp.seq, LANES))
                    fcum_t = jnp.swapaxes(fcum[:, :, :H_FOX], 1, 2)
                    o = _attn_prompt(q, k, v, fcum, fcum_t, grp.batch, grp.seq)
                else:
                    pool, page = cache_k.shape[1], cache_k.shape[2]
                    o = _attn_sample(q, k, v, logf,
                                     cache_k[i].reshape(pool, page, d_att), cache_v[i].reshape(pool, page, d_att),
                                     jnp.swapaxes(cache_logf[i], 1, 2), page_table)
                xs[gi] = _proj_residual(grp, o, w_o, x, mod[:, 2], "att_out")
                outs["k"][gi].append(k.reshape(grp.batch, grp.seq, H_ATT, HEAD_DIM))
                outs["v"][gi].append(v.reshape(grp.batch, grp.seq, H_ATT, HEAD_DIM))
                outs["lf"][gi].append(logf[:, :H_FOX].reshape(grp.batch, grp.seq, H_FOX))
        else:
            r_decay, r_aaa = rw_w1.shape[2], rw_a1.shape[2]
            w_rkv = jnp.concatenate([rw_w_r[i], rw_w_k[i], rw_w_v[i]], axis=1).astype(BF16)
            w_lr1 = jnp.concatenate([_pad_cols(rw_w1[i], LANES), _pad_cols(rw_a1[i], LANES), rw_g1[i]],
                                    axis=1).astype(BF16)
            n1 = w_lr1.shape[1]
            w_lr2 = jnp.zeros((n1, 3 * d), F32)
            w_lr2 = w_lr2.at[:r_decay, :d].set(rw_w2[i])
            w_lr2 = w_lr2.at[LANES:LANES + r_aaa, d:2 * d].set(rw_a2[i])
            w_lr2 = w_lr2.at[2 * LANES:, 2 * d:].set(rw_g2[i]).astype(BF16)
            bias_lr2 = jnp.concatenate([rw_w0[i], rw_a0[i], jnp.zeros((d,), F32)])[None, :]
            params = _pad_rows(jnp.stack([rw_k_k[i], rw_k_a[i], rw_r_k[i].reshape(d), rw_ln_w[i], rw_ln_b[i]]),
                               SUBLANES)
            w_o = rw_w_o[i].astype(BF16)
            for gi, grp in enumerate(groups):
                x, mod = xs[gi], mods[gi]
                h = _normmod(grp, x, norm_mix[l], mod[:, 0], mod[:, 1])
                h3 = h.reshape(grp.batch, grp.seq, d)
                if grp.seq > 1:
                    prev = jnp.concatenate([jnp.zeros((grp.batch, 1, d), F32), h3[:, :-1]], axis=1)
                else:
                    prev = state_shift[i][:, None, :]
                rkv, lag = _rwkv_project(grp, h, prev.reshape(grp.m, d), rw_mix[i], w_rkv, w_lr1, w_lr2, bias_lr2)
                if grp.seq > 1:
                    state0 = jnp.zeros((grp.batch, d // RWKV_HS, RWKV_HS, RWKV_HS), F32)
                    seq3 = lambda a: a.reshape(grp.batch, grp.seq, 3 * d)
                    z, s_fin = _rwkv_scan(seq3(rkv), seq3(lag), params, state0, grp.batch, grp.seq,
                                          min(64, grp.seq), 1)
                    z = z.reshape(grp.m, d)
                else:
                    z, s_fin = _rwkv_step(rkv, lag, params, state_wkv[i])
                xs[gi] = _proj_residual(grp, z, w_o, x, mod[:, 2], "rwkv_out")
                outs["wkv"][gi].append(s_fin)
                outs["shift"][gi].append(h3[:, -1])
        n_exp = peer_u.shape[1]
        slab = lambda w: w.reshape(n_exp, d // LANES, LANES)
        uv = jnp.concatenate([slab(peer_u[l]), slab(peer_v[l])], axis=1)
        w_q = peer_wq[l].astype(BF16)
        for gi, grp in enumerate(groups):
            xs[gi] = _peer(grp, xs[gi], mods[gi], norm_ffn[l], w_q, peer_k1[l], peer_k2[l], uv)

    st = lambda name, gi: jnp.stack(outs[name][gi])
    return (xs[0].reshape(bp, tp, d), xs[1].reshape(bs, 1, d),
            st("k", 0), st("v", 0), st("lf", 0), st("wkv", 0), st("shift", 0),
            st("k", 1), st("v", 1), st("lf", 1), st("wkv", 1), st("shift", 1))
```

```python
import functools

import jax
import jax.numpy as jnp
from jax import lax
from jax.experimental import pallas as pl
from jax.experimental.pallas import tpu as pltpu

F32 = jnp.float32
BF16 = jnp.bfloat16

HEAD_DIM = 128
H_FOX = 8
H_ATT = 16
ATT_SCALE = HEAD_DIM ** -0.5
NEG_INF = -1e30
RWKV_HS = 64
LNX_EPS = 64e-5
PEER_HEADS = 8
PEER_TOPK = 16
N_KEYS = 128
N_MOD = 6
RMS_EPS = 1e-6

LANES = 128
SUBLANES = 8
VMEM_LIMIT_BYTES = 56 * 1024 * 1024


def _cparams(sem):
    return pltpu.CompilerParams(dimension_semantics=sem, vmem_limit_bytes=VMEM_LIMIT_BYTES)


def _split_bf16(x, parts):
    out = []
    for _ in range(parts - 1):
        hi = x.astype(BF16)
        out.append(hi)
        x = x - hi.astype(F32)
    out.append(x.astype(BF16))
    return out


def _dot_exact_rhs(x, rhs_bf16, parts=3):
    acc = None
    for p in _split_bf16(x, parts):
        t = jnp.dot(p, rhs_bf16, preferred_element_type=F32)
        acc = t if acc is None else acc + t
    return acc


def _dot_exact_lhs(lhs_bf16, x, parts=3):
    acc = None
    for p in _split_bf16(x, parts):
        t = jnp.dot(lhs_bf16, p, preferred_element_type=F32)
        acc = t if acc is None else acc + t
    return acc


def _dotf(a, b):
    a3 = _split_bf16(a, 3)
    b3 = _split_bf16(b, 3)
    acc = None
    for i, j in ((2, 0), (0, 2), (1, 1), (1, 0), (0, 1), (0, 0)):
        t = jnp.dot(a3[i], b3[j], preferred_element_type=F32)
        acc = t if acc is None else acc + t
    return acc


def _dotf_nt(a, b):
    a3 = _split_bf16(a, 3)
    b3 = _split_bf16(b, 3)
    acc = None
    dn = (((1,), (1,)), ((), ()))
    for i, j in ((2, 0), (0, 2), (1, 1), (1, 0), (0, 1), (0, 0)):
        t = lax.dot_general(a3[i], b3[j], dn, preferred_element_type=F32)
        acc = t if acc is None else acc + t
    return acc


def _softplus(z):
    return jnp.maximum(z, 0.0) + jnp.log1p(jnp.exp(-jnp.abs(z)))


def _sigmoid(z):
    return 1.0 / (1.0 + jnp.exp(-z))


class _Group:
    def __init__(self, batch, seq, tm):
        self.batch, self.seq = batch, seq
        self.m = batch * seq
        if seq == 1:
            self.tm = min(tm, self.m)
        else:
            self.tm = min(tm, seq)
        assert self.m % self.tm == 0 and (seq == 1 or seq % self.tm == 0)
        self.tiles_per_seq = max(seq // self.tm, 1)

    def seq_vec(self, v):
        return v if self.seq == 1 else v[:, None, :]

    def seq_spec(self, width, col=None):
        col = col or (lambda *ids: 0)
        if self.seq == 1:
            return pl.BlockSpec((self.tm, width), lambda *ids: (ids[0], col(*ids)))
        tps = self.tiles_per_seq
        return pl.BlockSpec((None, 1, width), lambda *ids: (ids[0] // tps, 0, col(*ids)))


def _normmod_body(x_ref, g_ref, shift_ref, scale_ref, o_ref):
    x = x_ref[...]
    h = x * lax.rsqrt(jnp.mean(x * x, axis=-1, keepdims=True) + RMS_EPS) * g_ref[...]
    o_ref[...] = h * (1.0 + scale_ref[...]) + shift_ref[...]


def _normmod(grp, x, g, shift, scale):
    m, d = x.shape
    return pl.pallas_call(
        _normmod_body,
        grid=(m // grp.tm,),
        in_specs=[pl.BlockSpec((grp.tm, d), lambda i: (i, 0)),
                  pl.BlockSpec((1, d), lambda i: (0, 0)),
                  grp.seq_spec(d), grp.seq_spec(d)],
        out_specs=pl.BlockSpec((grp.tm, d), lambda i: (i, 0)),
        out_shape=jax.ShapeDtypeStruct((m, d), F32),
        compiler_params=_cparams(("parallel",)),
        name="normmod",
    )(x, g[None, :], grp.seq_vec(shift), grp.seq_vec(scale))


def _mm_body(*refs, n_a, n_e, n_out, prologue, epilogue, every_j):
    a_refs = refs[:n_a]
    w_ref = refs[n_a]
    e_refs = refs[n_a + 1:n_a + 1 + n_e]
    o_refs = refs[n_a + 1 + n_e:n_a + 1 + n_e + n_out]
    a_scr = refs[-1]
    j = pl.program_id(1)

    def fill():
        a_scr[...] = prologue(*[r[...] for r in a_refs]).astype(BF16)

    if every_j:
        fill()
    else:
        pl.when(j == 0)(fill)
    acc = jnp.dot(a_scr[...], w_ref[...].astype(BF16), preferred_element_type=F32)
    epilogue(acc, j, e_refs, o_refs)


def _mm(m, tm, w, tn, a_ins, a_specs, e_ins, e_specs, out_shapes, out_specs, prologue, epilogue, name,
        every_j=False):
    k, n = w.shape
    assert m % tm == 0 and n % tn == 0
    body = functools.partial(_mm_body, n_a=len(a_ins), n_e=len(e_ins), n_out=len(out_shapes),
                             prologue=prologue, epilogue=epilogue, every_j=every_j)
    return pl.pallas_call(
        body,
        grid=(m // tm, n // tn),
        in_specs=list(a_specs) + [pl.BlockSpec((k, tn), lambda i, j: (0, j))] + list(e_specs),
        out_specs=list(out_specs),
        out_shape=list(out_shapes),
        scratch_shapes=[pltpu.VMEM((tm, k), BF16)],
        compiler_params=_cparams(("parallel", "arbitrary")),
        name=name,
    )(*a_ins, w, *e_ins)


def _tile_spec(tm, width):
    return pl.BlockSpec((tm, width), lambda i, j: (i, 0))


def _out_spec(tm, tn):
    return pl.BlockSpec((tm, tn), lambda i, j: (i, j))


def _row_spec(tn):
    return pl.BlockSpec((1, tn), lambda i, j: (0, j))


def _ident(a):
    return a


def _epi(fn):
    def run(acc, j, e_refs, o_refs):
        o_refs[0][...] = fn(acc, j, *[r[...] for r in e_refs])
    return run


def _ada_mod(c, w, b):
    m, d = c.shape
    n = w.shape[1]
    tn = 512
    (out,) = _mm(m, m, w, tn, [c], [_tile_spec(m, d)], [b[None, :]], [_row_spec(tn)],
                 [jax.ShapeDtypeStruct((m, n), F32)], [_out_spec(m, tn)],
                 lambda cc: cc * _sigmoid(cc), _epi(lambda acc, j, bb: acc + bb), "ada_mod")
    return out


def _head_rmsnorm(y, gain):
    blocks = []
    for hh in range(y.shape[1] // HEAD_DIM):
        blk = y[:, hh * HEAD_DIM:(hh + 1) * HEAD_DIM]
        ms = jnp.mean(blk * blk, axis=-1, keepdims=True)
        blocks.append(blk * lax.rsqrt(ms + RMS_EPS) * gain)
    return jnp.concatenate(blocks, axis=1)


def _qkv_epilogue(acc, j, e_refs, o_refs):
    gains = e_refs[0]
    q_ref, k_ref, v_ref = o_refs

    @pl.when(j == 0)
    def _():
        q_ref[...] = _head_rmsnorm(acc, gains[0:1, :])

    @pl.when(j == 1)
    def _():
        q_ref[...] = acc

    @pl.when(j == 2)
    def _():
        k_ref[...] = _head_rmsnorm(acc, gains[1:2, :])

    @pl.when(j == 3)
    def _():
        k_ref[...] = acc

    @pl.when(j >= 4)
    def _():
        v_ref[...] = acc


def _att_project(grp, h, w_qkv, w_f, q_gain, k_gain, f_bias):
    m, d = h.shape
    tm = grp.tm
    d_att = H_ATT * HEAD_DIM
    tn = d_att // 2
    gains = jnp.stack([q_gain, k_gain])
    shp = jax.ShapeDtypeStruct((m, d_att), F32)

    def ospec(first):
        return pl.BlockSpec((tm, tn), lambda i, j: (i, jnp.clip(j - first, 0, 1)))

    q, k, v = _mm(m, tm, w_qkv, tn, [h], [_tile_spec(tm, d)],
                  [gains], [pl.BlockSpec((2, HEAD_DIM), lambda i, j: (0, 0))],
                  [shp, shp, shp], [ospec(0), ospec(2), ospec(4)],
                  _ident, _qkv_epilogue, "att_qkv")
    (logf,) = _mm(m, tm, w_f, LANES, [h], [_tile_spec(tm, d)],
                  [f_bias], [_row_spec(LANES)],
                  [jax.ShapeDtypeStruct((m, LANES), F32)], [_out_spec(tm, LANES)],
                  _ident, _epi(lambda acc, j, fb: -_softplus(-(acc + fb))), "att_logf")
    return q, k, v, logf


def _cumsum_body(x_ref, o_ref):
    t = x_ref.shape[0]
    row = lax.broadcasted_iota(jnp.int32, (t, t), 0)
    col = lax.broadcasted_iota(jnp.int32, (t, t), 1)
    tri = (col <= row).astype(BF16)
    o_ref[...] = _dot_exact_lhs(tri, x_ref[...])


def _cumsum_seq(x):
    b, t, w = x.shape
    return pl.pallas_call(
        _cumsum_body,
        grid=(b,),
        in_specs=[pl.BlockSpec((None, t, w), lambda i: (i, 0, 0))],
        out_specs=pl.BlockSpec((None, t, w), lambda i: (i, 0, 0)),
        out_shape=jax.ShapeDtypeStruct((b, t, w), F32),
        compiler_params=_cparams(("parallel",)),
        name="fox_cumsum",
    )(x)


def _attn_prompt_body(q_ref, k_ref, v_ref, fc_ref, ft_ref, o_ref, *, tq):
    h = pl.program_id(1)
    qi = pl.program_id(2)
    q = q_ref[...].astype(BF16)
    t_pos = qi * tq + lax.broadcasted_iota(jnp.int32, (tq, tq), 0)
    s_off = lax.broadcasted_iota(jnp.int32, (tq, tq), 1)
    nt = (((1,), (1,)), ((), ()))

    def scores(kb):
        start = pl.multiple_of(kb * tq, tq)
        kblk = k_ref[pl.ds(start, tq), :].astype(BF16)
        vblk = v_ref[pl.ds(start, tq), :].astype(BF16)
        s = lax.dot_general(q, kblk, nt, preferred_element_type=F32) * ATT_SCALE
        return s, vblk, start

    @pl.when(h < H_FOX)
    def _fox():
        lane = lax.broadcasted_iota(jnp.int32, fc_ref.shape, 1)
        fq = jnp.sum(jnp.where(lane == h, fc_ref[...], 0.0), axis=-1, keepdims=True)

        def step(kb, carry):
            m_run, l_run, acc = carry
            s, vblk, start = scores(kb)
            fk = ft_ref[pl.ds(h, 1), pl.ds(start, tq)]
            s = s + (fq - fk)
            s = jnp.where(kb * tq + s_off <= t_pos, s, NEG_INF)
            m_new = jnp.maximum(m_run, jnp.max(s, axis=-1, keepdims=True))
            alpha = jnp.exp(m_run - m_new)
            p = jnp.exp(s - m_new)
            l_new = alpha * l_run + jnp.sum(p, axis=-1, keepdims=True)
            acc = alpha * acc + jnp.dot(p.astype(BF16), vblk, preferred_element_type=F32)
            return m_new, l_new, acc

        init = (jnp.full((tq, 1), NEG_INF, F32), jnp.zeros((tq, 1), F32), jnp.zeros((tq, HEAD_DIM), F32))
        _, l_fin, acc = lax.fori_loop(0, qi + 1, step, init)
        o_ref[...] = acc / l_fin

    @pl.when(h >= H_FOX)
    def _sb():
        jj = lax.broadcasted_iota(jnp.int32, (tq, tq), 0)
        ss = lax.broadcasted_iota(jnp.int32, (tq, tq), 1)
        upper = (jj > ss).astype(BF16)

        def step(it, carry):
            later, acc = carry
            kb = qi - it
            z, vblk, _ = scores(kb)
            strict = kb * tq + s_off < t_pos
            lk = jnp.where(strict, -_softplus(z), 0.0)
            later_in = _dot_exact_rhs(lk, upper, parts=2)
            a = jnp.where(strict, jnp.exp(z + lk + later_in + later), 0.0)
            acc = acc + jnp.dot(a.astype(BF16), vblk, preferred_element_type=F32)
            later = later + jnp.sum(lk, axis=-1, keepdims=True)
            return later, acc

        init = (jnp.zeros((tq, 1), F32), jnp.zeros((tq, HEAD_DIM), F32))
        _, acc = lax.fori_loop(0, qi + 1, step, init)
        o_ref[...] = acc


def _attn_prompt(q, k, v, fcum, fcum_t, batch, seq):
    d_att = q.shape[1]
    tq = min(256, seq)
    q3, k3, v3 = (a.reshape(batch, seq, d_att) for a in (q, k, v))
    body = functools.partial(_attn_prompt_body, tq=tq)
    out = pl.pallas_call(
        body,
        grid=(batch, H_ATT, seq // tq),
        in_specs=[pl.BlockSpec((None, tq, HEAD_DIM), lambda b, h, i: (b, i, h)),
                  pl.BlockSpec((None, seq, HEAD_DIM), lambda b, h, i: (b, 0, h)),
                  pl.BlockSpec((None, seq, HEAD_DIM), lambda b, h, i: (b, 0, h)),
                  pl.BlockSpec((None, tq, LANES), lambda b, h, i: (b, i, 0)),
                  pl.BlockSpec((None, H_FOX, seq), lambda b, h, i: (b, 0, 0))],
        out_specs=pl.BlockSpec((None, tq, HEAD_DIM), lambda b, h, i: (b, i, h)),
        out_shape=jax.ShapeDtypeStruct((batch, seq, d_att), F32),
        compiler_params=_cparams(("parallel", "parallel", "arbitrary")),
        name="attn_prompt",
    )(q3, k3, v3, fcum, fcum_t)
    return out.reshape(batch * seq, d_att)


def _attn_sample_body(pt_ref, q_ref, kn_ref, vn_ref, lfn_ref, kc_ref, vc_ref, lfc_ref, o_ref,
                      qrows, m_scr, l_scr, suf_scr, acc_scr, *, n_pages):
    p = pl.program_id(1)
    d_att = q_ref.shape[-1]
    page = kc_ref.shape[0]
    sub = lax.broadcasted_iota(jnp.int32, (H_ATT, d_att), 0)
    lane_head = lax.broadcasted_iota(jnp.int32, (H_ATT, d_att), 1) // HEAD_DIM
    own = sub == lane_head
    is_fox = lax.broadcasted_iota(jnp.int32, (H_ATT, 1), 0) < H_FOX
    nt = (((1,), (1,)), ((), ()))

    @pl.when(p == 0)
    def _init():
        qr = jnp.where(own, q_ref[...], 0.0)
        qrows[...] = qr.astype(BF16)
        s_self = jnp.sum(qr * kn_ref[...], axis=-1, keepdims=True) * ATT_SCALE
        m_scr[...] = jnp.where(is_fox, s_self, 0.0)
        l_scr[...] = jnp.ones_like(l_scr)
        acc_scr[...] = jnp.where(sub < H_FOX, jnp.broadcast_to(vn_ref[...], (H_ATT, d_att)), 0.0)
        lf_lane = lax.broadcasted_iota(jnp.int32, (H_ATT, LANES), 1)
        lf_sub = lax.broadcasted_iota(jnp.int32, (H_ATT, LANES), 0)
        lf_new = jnp.sum(jnp.where(lf_lane == lf_sub, lfn_ref[...], 0.0), axis=-1, keepdims=True)
        suf_scr[...] = jnp.where(is_fox, lf_new, 0.0)

    s = lax.dot_general(qrows[...], kc_ref[...].astype(BF16), nt,
                        preferred_element_type=F32) * ATT_SCALE
    jj = lax.broadcasted_iota(jnp.int32, (page, page), 0)
    ss = lax.broadcasted_iota(jnp.int32, (page, page), 1)
    upper = (jj > ss).astype(BF16)
    lk = -_softplus(s)
    lf = jnp.concatenate([lfc_ref[...], jnp.zeros((H_ATT - H_FOX, page), F32)], axis=0)
    step_log = jnp.where(is_fox, lf, lk)
    later = _dot_exact_rhs(step_log, upper) + suf_scr[...]
    logit = s + later + jnp.where(is_fox, 0.0, lk)
    m_old = m_scr[...]
    m_new = jnp.where(is_fox, jnp.maximum(m_old, jnp.max(logit, axis=-1, keepdims=True)), 0.0)
    alpha = jnp.exp(m_old - m_new)
    w = jnp.exp(logit - m_new)
    l_scr[...] = jnp.where(is_fox, alpha * l_scr[...] + jnp.sum(w, axis=-1, keepdims=True), 1.0)
    acc_scr[...] = alpha * acc_scr[...] + jnp.dot(w.astype(BF16), vc_ref[...].astype(BF16),
                                                  preferred_element_type=F32)
    m_scr[...] = m_new
    suf_scr[...] = suf_scr[...] + jnp.sum(step_log, axis=-1, keepdims=True)

    @pl.when(p == n_pages - 1)
    def _fin():
        full = acc_scr[...] / l_scr[...]
        o_ref[...] = jnp.sum(jnp.where(own, full, 0.0), axis=0, keepdims=True)


def _attn_sample(q, k_new, v_new, logf_new, cache_k, cache_v, cache_lf_t, page_table):
    b, d_att = q.shape
    n_pages = page_table.shape[1]
    page = cache_k.shape[1]
    row = lambda a: a[:, None, :]

    def new_spec(width):
        return pl.BlockSpec((None, 1, width), lambda i, p, pt: (i, 0, 0))

    def page_spec(shape):
        return pl.BlockSpec((None,) + shape, lambda i, p, pt: (pt[i, n_pages - 1 - p], 0, 0))

    grid_spec = pltpu.PrefetchScalarGridSpec(
        num_scalar_prefetch=1,
        grid=(b, n_pages),
        in_specs=[new_spec(d_att), new_spec(d_att), new_spec(d_att), new_spec(LANES),
                  page_spec((page, d_att)), page_spec((page, d_att)), page_spec((H_FOX, page))],
        out_specs=pl.BlockSpec((None, 1, d_att), lambda i, p, pt: (i, 0, 0)),
        scratch_shapes=[pltpu.VMEM((H_ATT, d_att), BF16), pltpu.VMEM((H_ATT, 1), F32),
                        pltpu.VMEM((H_ATT, 1), F32), pltpu.VMEM((H_ATT, 1), F32),
                        pltpu.VMEM((H_ATT, d_att), F32)],
    )
    out = pl.pallas_call(
        functools.partial(_attn_sample_body, n_pages=n_pages),
        grid_spec=grid_spec,
        out_shape=jax.ShapeDtypeStruct((b, 1, d_att), F32),
        compiler_params=_cparams(("parallel", "arbitrary")),
        name="attn_sample",
    )(page_table, row(q), row(k_new), row(v_new), row(logf_new), cache_k, cache_v, cache_lf_t)
    return out.reshape(b, d_att)


def _attn_decode_body(pt_ref, q_ref, kn_ref, vn_ref, lfn_ref, kc_ref, vc_ref, lfc_ref, o_ref,
                      m_scr, l_scr, suf_scr, acc_scr, *, n_pages):
    p = pl.program_id(1)
    page = lfc_ref.shape[0]
    hd = HEAD_DIM
    lane = lax.broadcasted_iota(jnp.int32, (1, LANES), 1)
    lane_full = lax.broadcasted_iota(jnp.int32, (page, LANES), 1)
    is_fox = lane < H_FOX
    eye = (lax.broadcasted_iota(jnp.int32, (LANES, LANES), 0)
           == lax.broadcasted_iota(jnp.int32, (LANES, LANES), 1))

    def head_lanes(ref, h):
        return ref[:, h * hd:(h + 1) * hd]

    def to_col(row):
        return jnp.sum(jnp.where(eye, row, 0.0), axis=1, keepdims=True)

    @pl.when(p == 0)
    def _init():
        s_self = jnp.zeros((1, LANES), F32)
        for h in range(H_ATT):
            dot = jnp.sum(head_lanes(q_ref, h) * head_lanes(kn_ref, h), axis=1, keepdims=True)
            s_self = jnp.where(lane == h, dot * ATT_SCALE, s_self)
        m_scr[...] = jnp.where(is_fox, s_self, 0.0)
        l_scr[...] = jnp.ones_like(l_scr)
        v_new = jnp.concatenate([head_lanes(vn_ref, h) for h in range(H_ATT)], axis=0)
        acc_scr[...] = jnp.where(lax.broadcasted_iota(jnp.int32, (H_ATT, 1), 0) < H_FOX, v_new, 0.0)
        suf_scr[...] = jnp.where(is_fox, lfn_ref[...], 0.0)

    s = jnp.zeros((page, LANES), F32)
    for h in range(H_ATT):
        k_h = kc_ref[pl.ds(h, page, stride=H_ATT), :].astype(BF16)
        q_h = jnp.broadcast_to(head_lanes(q_ref, h), (SUBLANES, hd)).astype(BF16)
        col = lax.dot_general(k_h, q_h, _NT, preferred_element_type=F32)[:, 0:1]
        s = jnp.where(lane_full == h, col, s)
    s = s * ATT_SCALE
    lk = -_softplus(s)
    lf = jnp.concatenate([lfc_ref[...], jnp.zeros((page, LANES - H_FOX), F32)], axis=1)
    step_log = jnp.where(is_fox, lf, lk)
    newer = (lax.broadcasted_iota(jnp.int32, (page, page), 1)
             > lax.broadcasted_iota(jnp.int32, (page, page), 0)).astype(BF16)
    later = _dot_exact_lhs(newer, step_log) + suf_scr[...]
    logit = s + later + jnp.where(is_fox, 0.0, lk)
    m_old = m_scr[...]
    m_new = jnp.where(is_fox, jnp.maximum(m_old, jnp.max(logit, axis=0, keepdims=True)), 0.0)
    alpha = jnp.exp(m_old - m_new)
    w = jnp.exp(logit - m_new)
    l_scr[...] = jnp.where(is_fox, alpha * l_scr[...] + jnp.sum(w, axis=0, keepdims=True), 1.0)
    m_scr[...] = m_new
    suf_scr[...] = suf_scr[...] + jnp.sum(step_log, axis=0, keepdims=True)
    w_t = w.T
    pv = []
    for h in range(H_ATT):
        w_h = jnp.broadcast_to(w_t[h:h + 1, :], (SUBLANES, page)).astype(BF16)
        v_h = vc_ref[pl.ds(h, page, stride=H_ATT), :].astype(BF16)
        pv.append(jnp.dot(w_h, v_h, preferred_element_type=F32)[0:1, :])
    acc_scr[...] = to_col(alpha)[:H_ATT, :] * acc_scr[...] + jnp.concatenate(pv, axis=0)

    @pl.when(p == n_pages - 1)
    def _fin():
        o_ref[...] = acc_scr[...] / to_col(l_scr[...])[:H_ATT, :]


def _attn_decode(q, k_new, v_new, logf_new, cache_k, cache_v, cache_lf, page_table):
    b, d_att = q.shape
    n_pages = page_table.shape[1]
    page = cache_k.shape[1]
    assert page == LANES
    row = lambda a: a[:, None, :]
    rows = lambda c: c.reshape(c.shape[0], page * H_ATT, HEAD_DIM)

    def new_spec(width):
        return pl.BlockSpec((None, 1, width), lambda i, p, pt: (i, 0, 0))

    def page_spec(shape):
        zeros = (0,) * len(shape)
        return pl.BlockSpec((None,) + shape, lambda i, p, pt: (pt[i, n_pages - 1 - p],) + zeros)

    grid_spec = pltpu.PrefetchScalarGridSpec(
        num_scalar_prefetch=1,
        grid=(b, n_pages),
        in_specs=[new_spec(d_att), new_spec(d_att), new_spec(d_att), new_spec(LANES),
                  page_spec((page * H_ATT, HEAD_DIM)), page_spec((page * H_ATT, HEAD_DIM)),
                  page_spec((page, H_FOX))],
        out_specs=pl.BlockSpec((None, H_ATT, HEAD_DIM), lambda i, p, pt: (i, 0, 0)),
        scratch_shapes=[pltpu.VMEM((1, LANES), F32), pltpu.VMEM((1, LANES), F32), pltpu.VMEM((1, LANES), F32),
                        pltpu.VMEM((H_ATT, HEAD_DIM), F32)],
    )
    out = pl.pallas_call(
        functools.partial(_attn_decode_body, n_pages=n_pages),
        grid_spec=grid_spec,
        out_shape=jax.ShapeDtypeStruct((b, H_ATT, HEAD_DIM), F32),
        compiler_params=_cparams(("parallel", "arbitrary")),
        name="attn_decode",
    )(page_table, row(q), row(k_new), row(v_new), row(logf_new), rows(cache_k), rows(cache_v), cache_lf)
    return out.reshape(b, d_att)


def _proj_residual(grp, a, w, x, gate, name):
    m, d = x.shape
    tm, tn = grp.tm, min(1024, d)
    (out,) = _mm(m, tm, w, tn, [a], [_tile_spec(tm, a.shape[1])],
                 [x, grp.seq_vec(gate)], [_out_spec(tm, tn), grp.seq_spec(tn, col=lambda i, j: j)],
                 [jax.ShapeDtypeStruct((m, d), F32)], [_out_spec(tm, tn)],
                 _ident, _epi(lambda acc, j, xx, gg: xx + gg * acc), name)
    return out


def _mix_prologue(h, prev, mix):
    return h + (prev - h) * mix


def _rwkv_project(grp, h, prev, mix, w_rkv, w_lr1, w_lr2, bias_lr2):
    m, d = h.shape
    tm = grp.tm
    tn = d // 2
    mix_rkv = jnp.stack([mix[0], mix[2], mix[3]])[:, None, :]
    (rkv,) = _mm(m, tm, w_rkv, tn, [h, prev, mix_rkv],
                 [_tile_spec(tm, d), _tile_spec(tm, d),
                  pl.BlockSpec((None, 1, d), lambda i, j: (j // 2, 0, 0))],
                 [], [], [jax.ShapeDtypeStruct((m, 3 * d), F32)], [_out_spec(tm, tn)],
                 _mix_prologue, _epi(lambda acc, j: acc), "rwkv_rkv", every_j=True)

    mix_lr = jnp.stack([mix[1], mix[4], mix[5], mix[5]])[:, None, :]

    def lr1_act(acc, j):
        return lax.cond(j == 0, jnp.tanh, lambda a: lax.cond(j == 1, _ident, _sigmoid, a), acc)

    n1 = w_lr1.shape[1]
    (lr1,) = _mm(m, tm, w_lr1, LANES, [h, prev, mix_lr],
                 [_tile_spec(tm, d), _tile_spec(tm, d),
                  pl.BlockSpec((None, 1, d), lambda i, j: (j, 0, 0))],
                 [], [], [jax.ShapeDtypeStruct((m, n1), F32)], [_out_spec(tm, LANES)],
                 _mix_prologue, _epi(lr1_act), "rwkv_lr1", every_j=True)

    def lr2_act(acc, j, bias):
        y = acc + bias
        decay = lambda a: -jnp.exp(-_softplus(-a) - 0.5)
        return lax.cond(j < 2, decay, lambda a: lax.cond(j < 4, _sigmoid, _ident, a), y)

    (lag,) = _mm(m, tm, w_lr2, tn, [lr1], [_tile_spec(tm, n1)],
                 [bias_lr2], [_row_spec(tn)],
                 [jax.ShapeDtypeStruct((m, 3 * d), F32)], [_out_spec(tm, tn)],
                 _ident, _epi(lr2_act), "rwkv_lr2")
    return rkv, lag


def _dot3(a, b, dn):
    return lax.dot_general(a.astype(BF16), b.astype(BF16), dn, preferred_element_type=F32)


_NN = (((1,), (0,)), ((), ()))
_NT = (((1,), (1,)), ((), ()))


def _half_sum(x, lo_half):
    s0 = jnp.sum(jnp.where(lo_half, x, 0.0), axis=-1, keepdims=True)
    s1 = jnp.sum(jnp.where(lo_half, 0.0, x), axis=-1, keepdims=True)
    return jnp.where(lo_half, s0, s1)


def _rwkv_scan_body(r_ref, k_ref, v_ref, lw_ref, a_ref, g_ref, par_ref, s0_ref, z_ref, sT_ref,
                    s_scr, a2_scr, u0_scr, o0_scr, rh_scr, bp_scr, no_scr, kv_scr, last_scr,
                    *, chunk, n_chunks, n_pairs):
    hs = RWKV_HS
    blk = 2 * hs
    lane = lax.broadcasted_iota(jnp.int32, (1, blk), 1)
    lo_half = lane < hs
    hi_half = jnp.logical_not(lo_half)
    row = lax.broadcasted_iota(jnp.int32, (chunk, chunk), 0)
    col = lax.broadcasted_iota(jnp.int32, (chunk, chunk), 1)
    tril_incl = (col <= row)
    tril_strict = (col < row)
    eye = (col == row).astype(F32)
    rr = lax.broadcasted_iota(jnp.int32, (blk, blk), 0) < hs
    cc = lax.broadcasted_iota(jnp.int32, (blk, blk), 1) < hs
    same_head = rr == cc
    n_double = max(chunk.bit_length() - 2, 0)
    zero = jnp.zeros((hs, hs), F32)

    def lanes(pi):
        return slice(pi * blk, (pi + 1) * blk)

    def params(pi):
        return [par_ref[i:i + 1, lanes(pi)] for i in range(5)]

    grp_chunks = min(4, n_chunks)
    assert n_chunks % grp_chunks == 0
    bnn = (((2,), (1,)), ((0,), (0,)))
    bnt = (((2,), (2,)), ((0,), (0,)))

    def bdot(a, b, dn=bnn):
        return lax.dot_general(a.astype(BF16), b.astype(BF16), dn, preferred_element_type=F32)

    def prepare(gi, pi):
        g = grp_chunks
        c0 = gi * g
        sl = pl.ds(pl.multiple_of(c0 * chunk, g * chunk), g * chunk)
        r, k, v, lw, a = (ref[sl, lanes(pi)].reshape(g, chunk, blk)
                          for ref in (r_ref, k_ref, v_ref, lw_ref, a_ref))
        k_k, k_a = params(pi)[:2]
        kk = k * k_k
        kk = kk / jnp.maximum(jnp.sqrt(_half_sum(kk * kk, lo_half)), 1e-12)
        kmod = k * (1.0 + (a - 1.0) * k_a)
        tri = jnp.broadcast_to(tril_incl.astype(BF16), (g, chunk, chunk))
        cum = None
        for part in _split_bf16(lw, 3):
            t = lax.dot_general(tri, part, bnn, preferred_element_type=F32)
            cum = t if cum is None else cum + t
        dec_in = jnp.exp(cum)
        inv = jnp.exp(-cum)
        a_hat, b_hat, k_hat, r_hat = -kk * jnp.exp(cum - lw), kk * a * inv, kmod * inv, r * dec_in
        last = dec_in[:, chunk - 1:chunk, :]

        both = lambda x: jnp.concatenate([x, x], axis=0)
        split = lambda x: jnp.concatenate([jnp.where(lo_half, x, 0.0), jnp.where(hi_half, x, 0.0)], axis=0)
        merge = lambda x: jnp.where(lo_half, x[:g], x[g:])
        b2, k2, v2 = both(b_hat), both(k_hat), both(v)
        am = split(a_hat)
        n_mat = jnp.where(tril_strict, bdot(am, b2, bnt), 0.0)
        m_mat = jnp.where(tril_strict, bdot(am, k2, bnt), 0.0)
        inv_mat, pw = eye + n_mat, n_mat
        for _ in range(n_double):
            pw = bdot(pw, pw)
            inv_mat = inv_mat + bdot(inv_mat, pw)
        a2_scr[pi, pl.ds(c0, g)] = merge(bdot(inv_mat, both(a_hat)))
        u0_scr[pi, pl.ds(c0, g)] = merge(bdot(inv_mat, bdot(m_mat, v2)))
        rm = split(r_hat)
        no = jnp.where(tril_incl, bdot(rm, b2, bnt), 0.0)
        mo = jnp.where(tril_incl, bdot(rm, k2, bnt), 0.0)
        no_scr[pi, 0, pl.ds(c0, g)] = no[:g]
        no_scr[pi, 1, pl.ds(c0, g)] = no[g:]
        o0_scr[pi, pl.ds(c0, g)] = merge(bdot(mo, v2))
        rh_scr[pi, pl.ds(c0, g)] = r_hat
        b_last, k_last = b_hat * last, k_hat * last
        for j in range(g):
            bp_scr[pi, c0 + j] = b_last[j].T
            kv_scr[pi, c0 + j] = jnp.where(same_head, _dot3(k_last[j].T, v[j], _NN), 0.0)
            last_scr[pi, c0 + j] = jnp.broadcast_to(last[j], (blk, blk)).T

    def phase1(gi, _):
        for pi in range(n_pairs):
            prepare(gi, pi)
        return 0

    lax.fori_loop(0, n_chunks // grp_chunks, phase1, 0)

    for pi in range(n_pairs):
        s_scr[pi] = jnp.concatenate([jnp.concatenate([s0_ref[2 * pi], zero], axis=1),
                                     jnp.concatenate([zero, s0_ref[2 * pi + 1]], axis=1)], axis=0).T

    def advance(c, pi):
        sl = pl.ds(pl.multiple_of(c * chunk, chunk), chunk)
        state = s_scr[pi]
        u = u0_scr[pi, c] + _dot3(a2_scr[pi, c], state, _NN)
        s_scr[pi] = (jnp.where(same_head, state * last_scr[pi, c] + _dot3(bp_scr[pi, c], u, _NN), 0.0)
                     + kv_scr[pi, c])
        o = (_dot3(rh_scr[pi, c], state, _NN) + o0_scr[pi, c]
             + jnp.where(lo_half, _dot3(no_scr[pi, 0, c], u, _NN), _dot3(no_scr[pi, 1, c], u, _NN)))

        r, k, v, a, g = (ref[sl, lanes(pi)] for ref in (r_ref, k_ref, v_ref, a_ref, g_ref))
        _, k_a, r_k, ln_w, ln_b = params(pi)
        kmod = k * (1.0 + (a - 1.0) * k_a)
        mu = _half_sum(o, lo_half) / hs
        dev = o - mu
        var = _half_sum(dev * dev, lo_half) / hs
        o_n = dev * lax.rsqrt(var + LNX_EPS) * ln_w + ln_b
        bonus = _half_sum(r * kmod * r_k, lo_half) * v
        z_ref[sl, lanes(pi)] = (o_n + bonus) * g

    def phase2(c, _):
        for pi in range(n_pairs):
            advance(c, pi)
        return 0

    lax.fori_loop(0, n_chunks, phase2, 0)
    for pi in range(n_pairs):
        fin = s_scr[pi].T
        sT_ref[2 * pi] = fin[:hs, :hs]
        sT_ref[2 * pi + 1] = fin[hs:, hs:]


def _rwkv_scan(rkv, lag, params, state0, batch, seq, chunk, n_pairs):
    d = rkv.shape[-1] // 3
    blk = 2 * RWKV_HS
    width = n_pairs * blk
    groups = d // width
    n_chunks = seq // chunk

    def seq_spec(part):
        return pl.BlockSpec((None, seq, width), lambda b, p: (b, 0, part * groups + p))

    st_spec = pl.BlockSpec((None, 2 * n_pairs, RWKV_HS, RWKV_HS), lambda b, p: (b, p, 0, 0))
    body = functools.partial(_rwkv_scan_body, chunk=chunk, n_chunks=n_chunks, n_pairs=n_pairs)
    per_chunk = lambda *shape: pltpu.VMEM((n_pairs, n_chunks) + shape, F32)
    z, s_fin = pl.pallas_call(
        body,
        grid=(batch, groups),
        in_specs=[seq_spec(0), seq_spec(1), seq_spec(2), seq_spec(0), seq_spec(1), seq_spec(2),
                  pl.BlockSpec((SUBLANES, width), lambda b, p: (0, p)), st_spec],
        out_specs=[pl.BlockSpec((None, seq, width), lambda b, p: (b, 0, p)), st_spec],
        out_shape=[jax.ShapeDtypeStruct((batch, seq, d), F32),
                   jax.ShapeDtypeStruct(state0.shape, F32)],
        scratch_shapes=[pltpu.VMEM((n_pairs, blk, blk), F32),
                        per_chunk(chunk, blk), per_chunk(chunk, blk), per_chunk(chunk, blk),
                        per_chunk(chunk, blk), per_chunk(blk, chunk),
                        pltpu.VMEM((n_pairs, 2, n_chunks, chunk, chunk), F32),
                        per_chunk(blk, blk), per_chunk(blk, blk)],
        compiler_params=_cparams(("parallel", "parallel")),
        name="rwkv_scan",
    )(rkv, rkv, rkv, lag, lag, lag, params, state0)
    return z, s_fin


def _rwkv_step_body(rkv_ref, lag_ref, par_ref, s0_ref, z_ref, s1_ref):
    hs = RWKV_HS
    r, k, v = rkv_ref[0], rkv_ref[1], rkv_ref[2]
    lw, a, g = lag_ref[0], lag_ref[1], lag_ref[2]
    k_k, k_a, r_k, ln_w, ln_b = (par_ref[i] for i in range(5))
    eye = (lax.broadcasted_iota(jnp.int32, (hs, hs), 0) == lax.broadcasted_iota(jnp.int32, (hs, hs), 1))
    kk = k * k_k
    kk = kk / jnp.maximum(jnp.sqrt(jnp.sum(kk * kk, axis=-1, keepdims=True)), 1e-12)
    kmod = k * (1.0 + (a - 1.0) * k_a)
    state = s0_ref[...]
    sa = jnp.sum(state * (-kk), axis=-1, keepdims=True)
    v_col = jnp.sum(jnp.where(eye, v, 0.0), axis=-1, keepdims=True)
    new = state * jnp.exp(lw) + sa * (kk * a) + v_col * kmod
    s1_ref[...] = new
    o_col = jnp.sum(new * r, axis=-1, keepdims=True)
    o = jnp.sum(jnp.where(eye, o_col, 0.0), axis=1, keepdims=True)
    mu = jnp.mean(o, axis=-1, keepdims=True)
    dev = o - mu
    var = jnp.mean(dev * dev, axis=-1, keepdims=True)
    o_n = dev * lax.rsqrt(var + LNX_EPS) * ln_w + ln_b
    bonus = jnp.sum(r * kmod * r_k, axis=-1, keepdims=True) * v
    z_ref[...] = (o_n + bonus) * g


def _rwkv_step(rkv, lag, params, state0):
    b, heads, hs, _ = state0.shape
    vec = lambda x: x.reshape(b, 3, heads, 1, hs)
    vec_spec = pl.BlockSpec((None, 3, heads, 1, hs), lambda i: (i, 0, 0, 0, 0))
    st_spec = pl.BlockSpec((None, heads, hs, hs), lambda i: (i, 0, 0, 0))
    z, s_fin = pl.pallas_call(
        _rwkv_step_body,
        grid=(b,),
        in_specs=[vec_spec, vec_spec, pl.BlockSpec((SUBLANES, heads, 1, hs), lambda i: (0, 0, 0, 0)), st_spec],
        out_specs=[pl.BlockSpec((None, heads, 1, hs), lambda i: (i, 0, 0, 0)), st_spec],
        out_shape=[jax.ShapeDtypeStruct((b, heads, 1, hs), F32), jax.ShapeDtypeStruct(state0.shape, F32)],
        compiler_params=_cparams(("parallel",)),
        name="rwkv_step",
    )(vec(rkv), vec(lag), params.reshape(SUBLANES, heads, 1, hs), state0)
    return z.reshape(b, heads * hs), s_fin


def _peer_route_body(q_ref, k1_ref, k2_ref, ids_ref, gate_ref):
    tm = q_ref.shape[0]
    topk = PEER_TOPK
    n_cand = topk * topk
    half = k1_ref.shape[-1]
    key_row = lax.broadcasted_iota(jnp.int32, (N_KEYS, tm), 0).astype(F32)
    cand_row = lax.broadcasted_iota(jnp.int32, (n_cand, tm), 0).astype(F32)
    rank = lax.broadcasted_iota(jnp.int32, (topk, tm), 0)
    rank_f = rank.astype(F32)

    def take_max(s, rows, n_rows):
        m = jnp.max(s, axis=0, keepdims=True)
        idx = jnp.min(jnp.where(s == m, rows, float(n_rows)), axis=0, keepdims=True)
        return m, idx, jnp.where(rows == idx, -jnp.inf, s)

    vals_heads, ids_heads = [], []
    for hq in range(PEER_HEADS):
        q1 = q_ref[:, hq * 2 * half:hq * 2 * half + half].astype(BF16)
        q2 = q_ref[:, hq * 2 * half + half:(hq + 1) * 2 * half].astype(BF16)
        s1 = lax.dot_general(k1_ref[hq].astype(BF16), q1, _NT, preferred_element_type=F32)
        s2 = lax.dot_general(k2_ref[hq].astype(BF16), q2, _NT, preferred_element_type=F32)

        def sub_it(i, carry):
            s1, s2, v1, e1, v2, e2 = carry
            m1, i1, s1 = take_max(s1, key_row, N_KEYS)
            m2, i2, s2 = take_max(s2, key_row, N_KEYS)
            here = rank == i
            return (s1, s2, jnp.where(here, m1, v1), jnp.where(here, i1, e1),
                    jnp.where(here, m2, v2), jnp.where(here, i2, e2))

        zero = jnp.zeros((topk, tm), F32)
        _, _, v1, e1, v2, e2 = lax.fori_loop(0, topk, sub_it, (s1, s2, zero, zero, zero, zero))
        cs = jnp.concatenate([v1[i:i + 1, :] + v2 for i in range(topk)], axis=0)

        def cand_it(i, carry):
            cs, vals, ids = carry
            m, pos, cs = take_max(cs, cand_row, n_cand)
            hi = jnp.floor(pos * (1.0 / topk))
            lo = pos - hi * topk
            k_hi = jnp.sum(jnp.where(rank_f == hi, e1, 0.0), axis=0, keepdims=True)
            k_lo = jnp.sum(jnp.where(rank_f == lo, e2, 0.0), axis=0, keepdims=True)
            here = rank == i
            return cs, jnp.where(here, m, vals), jnp.where(here, k_hi * N_KEYS + k_lo, ids)

        _, vals, ids = lax.fori_loop(0, topk, cand_it, (cs, zero, zero))
        ex = jnp.exp(vals - jnp.max(vals, axis=0, keepdims=True))
        vals_heads.append(ex / jnp.sum(ex, axis=0, keepdims=True))
        ids_heads.append(ids)

    gate_ref[...] = jnp.concatenate(vals_heads, axis=0).T
    ids_ref[...] = jnp.concatenate(ids_heads, axis=0).T.astype(jnp.int32)


def _peer_route(q, k1, k2, tm):
    m, dq = q.shape
    assert PEER_HEADS * PEER_TOPK == N_KEYS
    kspec = pl.BlockSpec(k1.shape, lambda i: (0, 0, 0))
    ospec = pl.BlockSpec((tm, N_KEYS), lambda i: (i, 0))
    return pl.pallas_call(
        _peer_route_body,
        grid=(m // tm,),
        in_specs=[pl.BlockSpec((tm, dq), lambda i: (i, 0)), kspec, kspec],
        out_specs=[ospec, ospec],
        out_shape=[jax.ShapeDtypeStruct((m, N_KEYS), jnp.int32), jax.ShapeDtypeStruct((m, N_KEYS), F32)],
        compiler_params=_cparams(("parallel",)),
        name="peer_route",
    )(q, k1, k2)


def _gelu(x):
    return 0.5 * x * (1.0 + lax.erf(x * (2.0 ** -0.5)))


PEER_BUFFERS = 4


def _peer_expert_body(ids_ref, nxt_ref, h_ref, gate_ref, x_ref, g5_ref, uv_hbm, o_ref, *scratch, tt):
    nb = PEER_BUFFERS
    bufs, (sem, pout) = scratch[:nb], scratch[nb:]
    d = h_ref.shape[1]
    n_sel = gate_ref.shape[1]
    step = pl.program_id(0)
    last_step = pl.num_programs(0) - 1
    diag = (lax.broadcasted_iota(jnp.int32, (n_sel, n_sel), 0)
            == lax.broadcasted_iota(jnp.int32, (n_sel, n_sel), 1))

    n_rows = uv_hbm.shape[1] // 2
    seg = uv_hbm.shape[2]

    def issue(src_ref, row, slot):
        for p in range(n_sel):
            pltpu.make_async_copy(uv_hbm.at[src_ref[row, p]], bufs[slot].at[:, p], sem.at[slot]).start()

    def wait_all(slot):
        pltpu.make_async_copy(bufs[slot], bufs[slot], sem.at[slot]).wait()

    def compute(t, slot):
        part = None
        hrow = h_ref[pl.ds(t, 1), :]
        for r in range(n_rows):
            term = bufs[slot][r] * hrow[:, r * seg:(r + 1) * seg]
            part = term if part is None else part + term
        act = jnp.sum(part, axis=1, keepdims=True)
        gate = jnp.sum(jnp.where(diag, gate_ref[pl.ds(t, 1), :], 0.0), axis=1, keepdims=True)
        wgt = _gelu(act) * gate
        pout[pl.ds(t, 1), :] = jnp.concatenate(
            [jnp.sum(bufs[slot][n_rows + r] * wgt, axis=0, keepdims=True) for r in range(n_rows)], axis=1)

    ahead = nb - 1

    @pl.when(step == 0)
    def _():
        for j in range(ahead):
            issue(ids_ref, j, j)

    def token_group(k, _):
        for j in range(nb):
            t = nb * k + j
            issue(ids_ref, t + ahead, (j + ahead) % nb)
            wait_all(j)
            compute(t, j)
        return 0

    lax.fori_loop(0, tt // nb - 1, token_group, 0)
    for j in range(nb):
        t = tt - nb + j
        if t + ahead < tt:
            issue(ids_ref, t + ahead, (j + ahead) % nb)
        else:
            issue(nxt_ref, t + ahead - tt, (j + ahead) % nb)
        wait_all(j)
        compute(t, j)

    @pl.when(step == last_step)
    def _():
        for j in range(ahead):
            wait_all(j)

    o_ref[...] = x_ref[...] + g5_ref[...] * pout[...]


def _peer_experts(grp, h, ids, gates, x, gate5, uv, tt):
    m, d = h.shape
    n_sel = ids.shape[1]
    tt = min(tt, grp.tm)
    assert grp.tm % tt == 0 and tt % PEER_BUFFERS == 0
    sub = grp.tm // tt
    n_steps = m // tt
    ids_spec = lambda nxt: pl.BlockSpec((tt, n_sel), lambda i: (jnp.minimum(i + nxt, n_steps - 1), 0),
                                        memory_space=pltpu.SMEM)
    tile = lambda width: pl.BlockSpec((tt, width), lambda i: (i, 0))
    if grp.seq == 1:
        g5, g5_spec = gate5, tile(d)
    else:
        tps = grp.tiles_per_seq * sub
        g5, g5_spec = gate5[:, None, :], pl.BlockSpec((None, 1, d), lambda i: (i // tps, 0, 0))
    return pl.pallas_call(
        functools.partial(_peer_expert_body, tt=tt),
        grid=(n_steps,),
        in_specs=[ids_spec(0), ids_spec(1),
                  tile(d), tile(n_sel), tile(d), g5_spec,
                  pl.BlockSpec(memory_space=pl.ANY)],
        out_specs=tile(d),
        out_shape=jax.ShapeDtypeStruct((m, d), F32),
        scratch_shapes=[pltpu.VMEM((uv.shape[1], n_sel, uv.shape[2]), F32)] * PEER_BUFFERS
        + [pltpu.SemaphoreType.DMA((PEER_BUFFERS,)), pltpu.VMEM((tt, d), F32)],
        compiler_params=_cparams(("arbitrary",)),
        name="peer_experts",
    )(ids, ids, h, gates, x, g5, uv)


def _peer(grp, x, mod, norm_g, w_q, k1, k2, uv):
    h = _normmod(grp, x, norm_g, mod[:, 3], mod[:, 4])
    m, d = h.shape
    tn = min(1024, w_q.shape[1])
    (q,) = _mm(m, grp.tm, w_q, tn, [h], [_tile_spec(grp.tm, d)], [], [],
               [jax.ShapeDtypeStruct((m, w_q.shape[1]), F32)], [_out_spec(grp.tm, tn)],
               _ident, _epi(lambda acc, j: acc), "peer_q")
    ids, gates = _peer_route(q, k1, k2, min(128, m))
    return _peer_experts(grp, h, ids, gates, x, mod[:, 5], uv, 32)


def _pad_cols(w, width):
    return jnp.pad(w, ((0, 0), (0, width - w.shape[1])))


def _pad_rows(w, height):
    return jnp.pad(w, ((0, height - w.shape[0]), (0, 0)))


def kernel(x_prompt, x_sample, cache_k, cache_v, cache_logf, state_wkv, state_shift, page_table, c_prompt, c_sample, ada_w, ada_b, norm_mix, norm_ffn, att_w_in, att_w_o, att_q_gain, att_k_gain, att_f_bias, rw_mix, rw_w0, rw_w1, rw_w2, rw_a0, rw_a1, rw_a2, rw_g1, rw_g2, rw_k_k, rw_k_a, rw_r_k, rw_w_r, rw_w_k, rw_w_v, rw_w_o, rw_ln_w, rw_ln_b, peer_wq, peer_k1, peer_k2, peer_u, peer_v):
    bp, tp, d = x_prompt.shape
    bs = x_sample.shape[0]
    depth = ada_w.shape[0]
    d_att = H_ATT * HEAD_DIM
    groups = (_Group(bp, tp, 512), _Group(bs, 1, 128))
    xs = [x_prompt.reshape(bp * tp, d), x_sample.reshape(bs, d)]

    c_all = jnp.concatenate([c_prompt, c_sample], axis=0)
    n_seq = c_all.shape[0]
    c_all = jnp.pad(c_all, ((0, (-n_seq) % SUBLANES), (0, 0)))

    outs = {name: [[], []] for name in ("k", "v", "lf", "wkv", "shift")}
    for l in range(depth):
        mod_all = _ada_mod(c_all, ada_w[l], ada_b[l])[:n_seq].reshape(n_seq, N_MOD, d)
        mods = (mod_all[:bp], mod_all[bp:])
        i = l // 2
        if l % 2 == 0:
            w_qkv = att_w_in[i][:, :3 * d_att].astype(BF16)
            w_f = _pad_cols(att_w_in[i][:, 3 * d_att:], LANES).astype(BF16)
            f_bias = _pad_cols(att_f_bias[i][None, :], LANES)
            w_o = att_w_o[i].astype(BF16)
            for gi, grp in enumerate(groups):
                x, mod = xs[gi], mods[gi]
                h = _normmod(grp, x, norm_mix[l], mod[:, 0], mod[:, 1])
                q, k, v, logf = _att_project(grp, h, w_qkv, w_f, att_q_gain[i], att_k_gain[i], f_bias)
                if grp.seq > 1:
                    fcum = _cumsum_seq(logf.reshape(grp.batch, grp.seq, LANES))
                    fcum_t = jnp.swapaxes(fcum[:, :, :H_FOX], 1, 2)
                    o = _attn_prompt(q, k, v, fcum, fcum_t, grp.batch, grp.seq)
                else:
                    o = _attn_decode(q, k, v, logf, cache_k[i], cache_v[i], cache_logf[i], page_table)
                xs[gi] = _proj_residual(grp, o, w_o, x, mod[:, 2], "att_out")
                outs["k"][gi].append(k.reshape(grp.batch, grp.seq, H_ATT, HEAD_DIM))
                outs["v"][gi].append(v.reshape(grp.batch, grp.seq, H_ATT, HEAD_DIM))
                outs["lf"][gi].append(logf[:, :H_FOX].reshape(grp.batch, grp.seq, H_FOX))
        else:
            r_decay, r_aaa = rw_w1.shape[2], rw_a1.shape[2]
            w_rkv = jnp.concatenate([rw_w_r[i], rw_w_k[i], rw_w_v[i]], axis=1).astype(BF16)
            w_lr1 = jnp.concatenate([_pad_cols(rw_w1[i], LANES), _pad_cols(rw_a1[i], LANES), rw_g1[i]],
                                    axis=1).astype(BF16)
            n1 = w_lr1.shape[1]
            w_lr2 = jnp.zeros((n1, 3 * d), F32)
            w_lr2 = w_lr2.at[:r_decay, :d].set(rw_w2[i])
            w_lr2 = w_lr2.at[LANES:LANES + r_aaa, d:2 * d].set(rw_a2[i])
            w_lr2 = w_lr2.at[2 * LANES:, 2 * d:].set(rw_g2[i]).astype(BF16)
            bias_lr2 = jnp.concatenate([rw_w0[i], rw_a0[i], jnp.zeros((d,), F32)])[None, :]
            params = _pad_rows(jnp.stack([rw_k_k[i], rw_k_a[i], rw_r_k[i].reshape(d), rw_ln_w[i], rw_ln_b[i]]),
                               SUBLANES)
            w_o = rw_w_o[i].astype(BF16)
            for gi, grp in enumerate(groups):
                x, mod = xs[gi], mods[gi]
                h = _normmod(grp, x, norm_mix[l], mod[:, 0], mod[:, 1])
                h3 = h.reshape(grp.batch, grp.seq, d)
                if grp.seq > 1:
                    prev = jnp.concatenate([jnp.zeros((grp.batch, 1, d), F32), h3[:, :-1]], axis=1)
                else:
                    prev = state_shift[i][:, None, :]
                rkv, lag = _rwkv_project(grp, h, prev.reshape(grp.m, d), rw_mix[i], w_rkv, w_lr1, w_lr2, bias_lr2)
                if grp.seq > 1:
                    state0 = jnp.zeros((grp.batch, d // RWKV_HS, RWKV_HS, RWKV_HS), F32)
                    seq3 = lambda a: a.reshape(grp.batch, grp.seq, 3 * d)
                    z, s_fin = _rwkv_scan(seq3(rkv), seq3(lag), params, state0, grp.batch, grp.seq,
                                          min(64, grp.seq), 1)
                    z = z.reshape(grp.m, d)
                else:
                    z, s_fin = _rwkv_step(rkv, lag, params, state_wkv[i])
                xs[gi] = _proj_residual(grp, z, w_o, x, mod[:, 2], "rwkv_out")
                outs["wkv"][gi].append(s_fin)
                outs["shift"][gi].append(h3[:, -1])
        n_exp = peer_u.shape[1]
        slab = lambda w: w.reshape(n_exp, d // LANES, LANES)
        uv = jnp.concatenate([slab(peer_u[l]), slab(peer_v[l])], axis=1)
        w_q = peer_wq[l].astype(BF16)
        for gi, grp in enumerate(groups):
            xs[gi] = _peer(grp, xs[gi], mods[gi], norm_ffn[l], w_q, peer_k1[l], peer_k2[l], uv)

    st = lambda name, gi: jnp.stack(outs[name][gi])
    return (xs[0].reshape(bp, tp, d), xs[1].reshape(bs, 1, d),
            st("k", 0), st("v", 0), st("lf", 0), st("wkv", 0), st("shift", 0),
            st("k", 1), st("v", 1), st("lf", 1), st("wkv", 1), st("shift", 1))
```

```python
import functools

import jax
import jax.numpy as jnp
from jax import lax
from jax.experimental import pallas as pl
from jax.experimental.pallas import tpu as pltpu

F32 = jnp.float32
BF16 = jnp.bfloat16

HEAD_DIM = 128
H_FOX = 8
H_ATT = 16
ATT_SCALE = HEAD_DIM ** -0.5
NEG_INF = -1e30
RWKV_HS = 64
LNX_EPS = 64e-5
PEER_HEADS = 8
PEER_TOPK = 16
N_KEYS = 128
N_MOD = 6
RMS_EPS = 1e-6

LANES = 128
SUBLANES = 8
VMEM_LIMIT_BYTES = 56 * 1024 * 1024


def _cparams(sem):
    return pltpu.CompilerParams(dimension_semantics=sem, vmem_limit_bytes=VMEM_LIMIT_BYTES)


def _split_bf16(x, parts):
    out = []
    for _ in range(parts - 1):
        hi = x.astype(BF16)
        out.append(hi)
        x = x - hi.astype(F32)
    out.append(x.astype(BF16))
    return out


def _dot_exact_rhs(x, rhs_bf16, parts=3):
    acc = None
    for p in _split_bf16(x, parts):
        t = jnp.dot(p, rhs_bf16, preferred_element_type=F32)
        acc = t if acc is None else acc + t
    return acc


def _dot_exact_lhs(lhs_bf16, x, parts=3):
    acc = None
    for p in _split_bf16(x, parts):
        t = jnp.dot(lhs_bf16, p, preferred_element_type=F32)
        acc = t if acc is None else acc + t
    return acc


def _softplus(z):
    return jnp.maximum(z, 0.0) + jnp.log1p(jnp.exp(-jnp.abs(z)))


def _sigmoid(z):
    return 1.0 / (1.0 + jnp.exp(-z))


class _Group:
    def __init__(self, batch, seq, tm):
        self.batch, self.seq = batch, seq
        self.m = batch * seq
        if seq == 1:
            self.tm = min(tm, self.m)
        else:
            self.tm = min(tm, seq)
        assert self.m % self.tm == 0 and (seq == 1 or seq % self.tm == 0)
        self.tiles_per_seq = max(seq // self.tm, 1)

    def seq_vec(self, v):
        return v if self.seq == 1 else v[:, None, :]

    def seq_spec(self, width, col=None):
        col = col or (lambda *ids: 0)
        if self.seq == 1:
            return pl.BlockSpec((self.tm, width), lambda *ids: (ids[0], col(*ids)))
        tps = self.tiles_per_seq
        return pl.BlockSpec((None, 1, width), lambda *ids: (ids[0] // tps, 0, col(*ids)))


def _normmod_body(x_ref, g_ref, shift_ref, scale_ref, o_ref):
    x = x_ref[...]
    h = x * lax.rsqrt(jnp.mean(x * x, axis=-1, keepdims=True) + RMS_EPS) * g_ref[...]
    o_ref[...] = h * (1.0 + scale_ref[...]) + shift_ref[...]


def _normmod(grp, x, g, shift, scale):
    m, d = x.shape
    return pl.pallas_call(
        _normmod_body,
        grid=(m // grp.tm,),
        in_specs=[pl.BlockSpec((grp.tm, d), lambda i: (i, 0)),
                  pl.BlockSpec((1, d), lambda i: (0, 0)),
                  grp.seq_spec(d), grp.seq_spec(d)],
        out_specs=pl.BlockSpec((grp.tm, d), lambda i: (i, 0)),
        out_shape=jax.ShapeDtypeStruct((m, d), F32),
        compiler_params=_cparams(("parallel",)),
        name="normmod",
    )(x, g[None, :], grp.seq_vec(shift), grp.seq_vec(scale))


def _mm_body(*refs, n_a, n_e, n_out, prologue, epilogue, every_j):
    a_refs = refs[:n_a]
    w_ref = refs[n_a]
    e_refs = refs[n_a + 1:n_a + 1 + n_e]
    o_refs = refs[n_a + 1 + n_e:n_a + 1 + n_e + n_out]
    a_scr = refs[-1]
    j = pl.program_id(1)

    def fill():
        a_scr[...] = prologue(*[r[...] for r in a_refs]).astype(BF16)

    if every_j:
        fill()
    else:
        pl.when(j == 0)(fill)
    acc = jnp.dot(a_scr[...], w_ref[...].astype(BF16), preferred_element_type=F32)
    epilogue(acc, j, e_refs, o_refs)


def _mm(m, tm, w, tn, a_ins, a_specs, e_ins, e_specs, out_shapes, out_specs, prologue, epilogue, name,
        every_j=False):
    k, n = w.shape
    assert m % tm == 0 and n % tn == 0
    body = functools.partial(_mm_body, n_a=len(a_ins), n_e=len(e_ins), n_out=len(out_shapes),
                             prologue=prologue, epilogue=epilogue, every_j=every_j)
    return pl.pallas_call(
        body,
        grid=(m // tm, n // tn),
        in_specs=list(a_specs) + [pl.BlockSpec((k, tn), lambda i, j: (0, j))] + list(e_specs),
        out_specs=list(out_specs),
        out_shape=list(out_shapes),
        scratch_shapes=[pltpu.VMEM((tm, k), BF16)],
        compiler_params=_cparams(("parallel", "arbitrary")),
        name=name,
    )(*a_ins, w, *e_ins)


def _tile_spec(tm, width):
    return pl.BlockSpec((tm, width), lambda i, j: (i, 0))


def _out_spec(tm, tn):
    return pl.BlockSpec((tm, tn), lambda i, j: (i, j))


def _row_spec(tn):
    return pl.BlockSpec((1, tn), lambda i, j: (0, j))


def _ident(a):
    return a


def _epi(fn):
    def run(acc, j, e_refs, o_refs):
        o_refs[0][...] = fn(acc, j, *[r[...] for r in e_refs])
    return run


def _ada_mod(c, w, b):
    m, d = c.shape
    n = w.shape[1]
    tn = 512
    (out,) = _mm(m, m, w, tn, [c], [_tile_spec(m, d)], [b[None, :]], [_row_spec(tn)],
                 [jax.ShapeDtypeStruct((m, n), F32)], [_out_spec(m, tn)],
                 lambda cc: cc * _sigmoid(cc), _epi(lambda acc, j, bb: acc + bb), "ada_mod")
    return out


def _head_rmsnorm(y, gain):
    blocks = []
    for hh in range(y.shape[1] // HEAD_DIM):
        blk = y[:, hh * HEAD_DIM:(hh + 1) * HEAD_DIM]
        ms = jnp.mean(blk * blk, axis=-1, keepdims=True)
        blocks.append(blk * lax.rsqrt(ms + RMS_EPS) * gain)
    return jnp.concatenate(blocks, axis=1)


def _qkv_epilogue(acc, j, e_refs, o_refs):
    gains = e_refs[0]
    q_ref, k_ref, v_ref = o_refs

    @pl.when(j == 0)
    def _():
        q_ref[...] = _head_rmsnorm(acc, gains[0:1, :])

    @pl.when(j == 1)
    def _():
        q_ref[...] = acc

    @pl.when(j == 2)
    def _():
        k_ref[...] = _head_rmsnorm(acc, gains[1:2, :])

    @pl.when(j == 3)
    def _():
        k_ref[...] = acc

    @pl.when(j >= 4)
    def _():
        v_ref[...] = acc


def _att_project(grp, h, w_qkv, w_f, q_gain, k_gain, f_bias):
    m, d = h.shape
    tm = grp.tm
    d_att = H_ATT * HEAD_DIM
    tn = d_att // 2
    gains = jnp.stack([q_gain, k_gain])
    shp = jax.ShapeDtypeStruct((m, d_att), F32)

    def ospec(first):
        return pl.BlockSpec((tm, tn), lambda i, j: (i, jnp.clip(j - first, 0, 1)))

    q, k, v = _mm(m, tm, w_qkv, tn, [h], [_tile_spec(tm, d)],
                  [gains], [pl.BlockSpec((2, HEAD_DIM), lambda i, j: (0, 0))],
                  [shp, shp, shp], [ospec(0), ospec(2), ospec(4)],
                  _ident, _qkv_epilogue, "att_qkv")
    (logf,) = _mm(m, tm, w_f, LANES, [h], [_tile_spec(tm, d)],
                  [f_bias], [_row_spec(LANES)],
                  [jax.ShapeDtypeStruct((m, LANES), F32)], [_out_spec(tm, LANES)],
                  _ident, _epi(lambda acc, j, fb: -_softplus(-(acc + fb))), "att_logf")
    return q, k, v, logf


def _cumsum_body(x_ref, o_ref):
    t = x_ref.shape[0]
    row = lax.broadcasted_iota(jnp.int32, (t, t), 0)
    col = lax.broadcasted_iota(jnp.int32, (t, t), 1)
    tri = (col <= row).astype(BF16)
    o_ref[...] = _dot_exact_lhs(tri, x_ref[...])


def _cumsum_seq(x):
    b, t, w = x.shape
    return pl.pallas_call(
        _cumsum_body,
        grid=(b,),
        in_specs=[pl.BlockSpec((None, t, w), lambda i: (i, 0, 0))],
        out_specs=pl.BlockSpec((None, t, w), lambda i: (i, 0, 0)),
        out_shape=jax.ShapeDtypeStruct((b, t, w), F32),
        compiler_params=_cparams(("parallel",)),
        name="fox_cumsum",
    )(x)


def _attn_prompt_body(q_ref, k_ref, v_ref, fc_ref, ft_ref, o_ref, *, tq):
    h = pl.program_id(1)
    qi = pl.program_id(2)
    q = q_ref[...].astype(BF16)
    t_pos = qi * tq + lax.broadcasted_iota(jnp.int32, (tq, tq), 0)
    s_off = lax.broadcasted_iota(jnp.int32, (tq, tq), 1)
    nt = (((1,), (1,)), ((), ()))

    def scores(kb):
        start = pl.multiple_of(kb * tq, tq)
        kblk = k_ref[pl.ds(start, tq), :].astype(BF16)
        vblk = v_ref[pl.ds(start, tq), :].astype(BF16)
        s = lax.dot_general(q, kblk, nt, preferred_element_type=F32) * ATT_SCALE
        return s, vblk, start

    @pl.when(h < H_FOX)
    def _fox():
        lane = lax.broadcasted_iota(jnp.int32, fc_ref.shape, 1)
        fq = jnp.sum(jnp.where(lane == h, fc_ref[...], 0.0), axis=-1, keepdims=True)

        def step(kb, carry):
            m_run, l_run, acc = carry
            s, vblk, start = scores(kb)
            fk = ft_ref[pl.ds(h, 1), pl.ds(start, tq)]
            s = s + (fq - fk)
            s = jnp.where(kb * tq + s_off <= t_pos, s, NEG_INF)
            m_new = jnp.maximum(m_run, jnp.max(s, axis=-1, keepdims=True))
            alpha = jnp.exp(m_run - m_new)
            p = jnp.exp(s - m_new)
            l_new = alpha * l_run + jnp.sum(p, axis=-1, keepdims=True)
            acc = alpha * acc + jnp.dot(p.astype(BF16), vblk, preferred_element_type=F32)
            return m_new, l_new, acc

        init = (jnp.full((tq, 1), NEG_INF, F32), jnp.zeros((tq, 1), F32), jnp.zeros((tq, HEAD_DIM), F32))
        _, l_fin, acc = lax.fori_loop(0, qi + 1, step, init)
        o_ref[...] = acc / l_fin

    @pl.when(h >= H_FOX)
    def _sb():
        jj = lax.broadcasted_iota(jnp.int32, (tq, tq), 0)
        ss = lax.broadcasted_iota(jnp.int32, (tq, tq), 1)
        upper = (jj > ss).astype(BF16)

        def step(it, carry):
            later, acc = carry
            kb = qi - it
            z, vblk, _ = scores(kb)
            strict = kb * tq + s_off < t_pos
            lk = jnp.where(strict, -_softplus(z), 0.0)
            later_in = _dot_exact_rhs(lk, upper, parts=2)
            a = jnp.where(strict, jnp.exp(z + lk + later_in + later), 0.0)
            acc = acc + jnp.dot(a.astype(BF16), vblk, preferred_element_type=F32)
            later = later + jnp.sum(lk, axis=-1, keepdims=True)
            return later, acc

        init = (jnp.zeros((tq, 1), F32), jnp.zeros((tq, HEAD_DIM), F32))
        _, acc = lax.fori_loop(0, qi + 1, step, init)
        o_ref[...] = acc


def _attn_prompt(q, k, v, fcum, fcum_t, batch, seq):
    d_att = q.shape[1]
    tq = min(256, seq)
    q3, k3, v3 = (a.reshape(batch, seq, d_att) for a in (q, k, v))
    body = functools.partial(_attn_prompt_body, tq=tq)
    out = pl.pallas_call(
        body,
        grid=(batch, H_ATT, seq // tq),
        in_specs=[pl.BlockSpec((None, tq, HEAD_DIM), lambda b, h, i: (b, i, h)),
                  pl.BlockSpec((None, seq, HEAD_DIM), lambda b, h, i: (b, 0, h)),
                  pl.BlockSpec((None, seq, HEAD_DIM), lambda b, h, i: (b, 0, h)),
                  pl.BlockSpec((None, tq, LANES), lambda b, h, i: (b, i, 0)),
                  pl.BlockSpec((None, H_FOX, seq), lambda b, h, i: (b, 0, 0))],
        out_specs=pl.BlockSpec((None, tq, HEAD_DIM), lambda b, h, i: (b, i, h)),
        out_shape=jax.ShapeDtypeStruct((batch, seq, d_att), F32),
        compiler_params=_cparams(("parallel", "parallel", "arbitrary")),
        name="attn_prompt",
    )(q3, k3, v3, fcum, fcum_t)
    return out.reshape(batch * seq, d_att)


def _attn_decode_body(pt_ref, q_ref, kn_ref, vn_ref, lfn_ref, kc_ref, vc_ref, lfc_ref, o_ref,
                      m_scr, l_scr, suf_scr, acc_scr, *, n_pages):
    p = pl.program_id(1)
    page = lfc_ref.shape[0]
    hd = HEAD_DIM
    lane = lax.broadcasted_iota(jnp.int32, (1, LANES), 1)
    lane_full = lax.broadcasted_iota(jnp.int32, (page, LANES), 1)
    is_fox = lane < H_FOX
    eye = (lax.broadcasted_iota(jnp.int32, (LANES, LANES), 0)
           == lax.broadcasted_iota(jnp.int32, (LANES, LANES), 1))

    def head_lanes(ref, h):
        return ref[:, h * hd:(h + 1) * hd]

    def to_col(row):
        return jnp.sum(jnp.where(eye, row, 0.0), axis=1, keepdims=True)

    @pl.when(p == 0)
    def _init():
        s_self = jnp.zeros((1, LANES), F32)
        for h in range(H_ATT):
            dot = jnp.sum(head_lanes(q_ref, h) * head_lanes(kn_ref, h), axis=1, keepdims=True)
            s_self = jnp.where(lane == h, dot * ATT_SCALE, s_self)
        m_scr[...] = jnp.where(is_fox, s_self, 0.0)
        l_scr[...] = jnp.ones_like(l_scr)
        v_new = jnp.concatenate([head_lanes(vn_ref, h) for h in range(H_ATT)], axis=0)
        acc_scr[...] = jnp.where(lax.broadcasted_iota(jnp.int32, (H_ATT, 1), 0) < H_FOX, v_new, 0.0)
        suf_scr[...] = jnp.where(is_fox, lfn_ref[...], 0.0)

    s = jnp.zeros((page, LANES), F32)
    for h in range(H_ATT):
        k_h = kc_ref[pl.ds(h, page, stride=H_ATT), :].astype(BF16)
        q_h = jnp.broadcast_to(head_lanes(q_ref, h), (SUBLANES, hd)).astype(BF16)
        col = lax.dot_general(k_h, q_h, _NT, preferred_element_type=F32)[:, 0:1]
        s = jnp.where(lane_full == h, col, s)
    s = s * ATT_SCALE
    lk = -_softplus(s)
    lf = jnp.concatenate([lfc_ref[...], jnp.zeros((page, LANES - H_FOX), F32)], axis=1)
    step_log = jnp.where(is_fox, lf, lk)
    newer = (lax.broadcasted_iota(jnp.int32, (page, page), 1)
             > lax.broadcasted_iota(jnp.int32, (page, page), 0)).astype(BF16)
    later = _dot_exact_lhs(newer, step_log) + suf_scr[...]
    logit = s + later + jnp.where(is_fox, 0.0, lk)
    m_old = m_scr[...]
    m_new = jnp.where(is_fox, jnp.maximum(m_old, jnp.max(logit, axis=0, keepdims=True)), 0.0)
    alpha = jnp.exp(m_old - m_new)
    w = jnp.exp(logit - m_new)
    l_scr[...] = jnp.where(is_fox, alpha * l_scr[...] + jnp.sum(w, axis=0, keepdims=True), 1.0)
    m_scr[...] = m_new
    suf_scr[...] = suf_scr[...] + jnp.sum(step_log, axis=0, keepdims=True)
    w_t = w.T
    pv = []
    for h in range(H_ATT):
        w_h = jnp.broadcast_to(w_t[h:h + 1, :], (SUBLANES, page)).astype(BF16)
        v_h = vc_ref[pl.ds(h, page, stride=H_ATT), :].astype(BF16)
        pv.append(jnp.dot(w_h, v_h, preferred_element_type=F32)[0:1, :])
    acc_scr[...] = to_col(alpha)[:H_ATT, :] * acc_scr[...] + jnp.concatenate(pv, axis=0)

    @pl.when(p == n_pages - 1)
    def _fin():
        o_ref[...] = acc_scr[...] / to_col(l_scr[...])[:H_ATT, :]


def _attn_decode(q, k_new, v_new, logf_new, cache_k, cache_v, cache_lf, page_table):
    b, d_att = q.shape
    n_pages = page_table.shape[1]
    page = cache_k.shape[1]
    assert page == LANES
    row = lambda a: a[:, None, :]
    rows = lambda c: c.reshape(c.shape[0], page * H_ATT, HEAD_DIM)

    def new_spec(width):
        return pl.BlockSpec((None, 1, width), lambda i, p, pt: (i, 0, 0))

    def page_spec(shape):
        zeros = (0,) * len(shape)
        return pl.BlockSpec((None,) + shape, lambda i, p, pt: (pt[i, n_pages - 1 - p],) + zeros)

    grid_spec = pltpu.PrefetchScalarGridSpec(
        num_scalar_prefetch=1,
        grid=(b, n_pages),
        in_specs=[new_spec(d_att), new_spec(d_att), new_spec(d_att), new_spec(LANES),
                  page_spec((page * H_ATT, HEAD_DIM)), page_spec((page * H_ATT, HEAD_DIM)),
                  page_spec((page, H_FOX))],
        out_specs=pl.BlockSpec((None, H_ATT, HEAD_DIM), lambda i, p, pt: (i, 0, 0)),
        scratch_shapes=[pltpu.VMEM((1, LANES), F32), pltpu.VMEM((1, LANES), F32), pltpu.VMEM((1, LANES), F32),
                        pltpu.VMEM((H_ATT, HEAD_DIM), F32)],
    )
    out = pl.pallas_call(
        functools.partial(_attn_decode_body, n_pages=n_pages),
        grid_spec=grid_spec,
        out_shape=jax.ShapeDtypeStruct((b, H_ATT, HEAD_DIM), F32),
        compiler_params=_cparams(("parallel", "arbitrary")),
        name="attn_decode",
    )(page_table, row(q), row(k_new), row(v_new), row(logf_new), rows(cache_k), rows(cache_v), cache_lf)
    return out.reshape(b, d_att)


def _proj_residual(grp, a, w, x, gate, name):
    m, d = x.shape
    tm, tn = grp.tm, min(1024, d)
    (out,) = _mm(m, tm, w, tn, [a], [_tile_spec(tm, a.shape[1])],
                 [x, grp.seq_vec(gate)], [_out_spec(tm, tn), grp.seq_spec(tn, col=lambda i, j: j)],
                 [jax.ShapeDtypeStruct((m, d), F32)], [_out_spec(tm, tn)],
                 _ident, _epi(lambda acc, j, xx, gg: xx + gg * acc), name)
    return out


def _mix_prologue(h, prev, mix):
    return h + (prev - h) * mix


def _rwkv_project(grp, h, prev, mix, w_rkv, w_lr1, w_lr2, bias_lr2):
    m, d = h.shape
    tm = grp.tm
    tn = d // 2
    mix_rkv = jnp.stack([mix[0], mix[2], mix[3]])[:, None, :]
    (rkv,) = _mm(m, tm, w_rkv, tn, [h, prev, mix_rkv],
                 [_tile_spec(tm, d), _tile_spec(tm, d),
                  pl.BlockSpec((None, 1, d), lambda i, j: (j // 2, 0, 0))],
                 [], [], [jax.ShapeDtypeStruct((m, 3 * d), F32)], [_out_spec(tm, tn)],
                 _mix_prologue, _epi(lambda acc, j: acc), "rwkv_rkv", every_j=True)

    mix_lr = jnp.stack([mix[1], mix[4], mix[5], mix[5]])[:, None, :]

    def lr1_act(acc, j):
        return lax.cond(j == 0, jnp.tanh, lambda a: lax.cond(j == 1, _ident, _sigmoid, a), acc)

    n1 = w_lr1.shape[1]
    (lr1,) = _mm(m, tm, w_lr1, LANES, [h, prev, mix_lr],
                 [_tile_spec(tm, d), _tile_spec(tm, d),
                  pl.BlockSpec((None, 1, d), lambda i, j: (j, 0, 0))],
                 [], [], [jax.ShapeDtypeStruct((m, n1), F32)], [_out_spec(tm, LANES)],
                 _mix_prologue, _epi(lr1_act), "rwkv_lr1", every_j=True)

    def lr2_act(acc, j, bias):
        y = acc + bias
        decay = lambda a: -jnp.exp(-_softplus(-a) - 0.5)
        return lax.cond(j < 2, decay, lambda a: lax.cond(j < 4, _sigmoid, _ident, a), y)

    (lag,) = _mm(m, tm, w_lr2, tn, [lr1], [_tile_spec(tm, n1)],
                 [bias_lr2], [_row_spec(tn)],
                 [jax.ShapeDtypeStruct((m, 3 * d), F32)], [_out_spec(tm, tn)],
                 _ident, _epi(lr2_act), "rwkv_lr2")
    return rkv, lag


def _dot3(a, b, dn):
    return lax.dot_general(a.astype(BF16), b.astype(BF16), dn, preferred_element_type=F32)


_NN = (((1,), (0,)), ((), ()))
_NT = (((1,), (1,)), ((), ()))


def _half_sum(x, lo_half):
    s0 = jnp.sum(jnp.where(lo_half, x, 0.0), axis=-1, keepdims=True)
    s1 = jnp.sum(jnp.where(lo_half, 0.0, x), axis=-1, keepdims=True)
    return jnp.where(lo_half, s0, s1)


def _rwkv_scan_body(r_ref, k_ref, v_ref, lw_ref, a_ref, g_ref, par_ref, s0_ref, z_ref, sT_ref,
                    s_scr, a2_scr, u0_scr, o0_scr, rh_scr, bp_scr, no_scr, kv_scr, last_scr,
                    *, chunk, n_chunks, n_pairs):
    hs = RWKV_HS
    blk = 2 * hs
    lane = lax.broadcasted_iota(jnp.int32, (1, blk), 1)
    lo_half = lane < hs
    hi_half = jnp.logical_not(lo_half)
    row = lax.broadcasted_iota(jnp.int32, (chunk, chunk), 0)
    col = lax.broadcasted_iota(jnp.int32, (chunk, chunk), 1)
    tril_incl = (col <= row)
    tril_strict = (col < row)
    eye = (col == row).astype(F32)
    rr = lax.broadcasted_iota(jnp.int32, (blk, blk), 0) < hs
    cc = lax.broadcasted_iota(jnp.int32, (blk, blk), 1) < hs
    same_head = rr == cc
    n_double = max(chunk.bit_length() - 2, 0)
    zero = jnp.zeros((hs, hs), F32)

    def lanes(pi):
        return slice(pi * blk, (pi + 1) * blk)

    def params(pi):
        return [par_ref[i:i + 1, lanes(pi)] for i in range(5)]

    grp_chunks = min(4, n_chunks)
    assert n_chunks % grp_chunks == 0
    bnn = (((2,), (1,)), ((0,), (0,)))
    bnt = (((2,), (2,)), ((0,), (0,)))

    def bdot(a, b, dn=bnn):
        return lax.dot_general(a.astype(BF16), b.astype(BF16), dn, preferred_element_type=F32)

    def prepare(gi, pi):
        g = grp_chunks
        c0 = gi * g
        sl = pl.ds(pl.multiple_of(c0 * chunk, g * chunk), g * chunk)
        r, k, v, lw, a = (ref[sl, lanes(pi)].reshape(g, chunk, blk)
                          for ref in (r_ref, k_ref, v_ref, lw_ref, a_ref))
        k_k, k_a = params(pi)[:2]
        kk = k * k_k
        kk = kk / jnp.maximum(jnp.sqrt(_half_sum(kk * kk, lo_half)), 1e-12)
        kmod = k * (1.0 + (a - 1.0) * k_a)
        tri = jnp.broadcast_to(tril_incl.astype(BF16), (g, chunk, chunk))
        cum = None
        for part in _split_bf16(lw, 3):
            t = lax.dot_general(tri, part, bnn, preferred_element_type=F32)
            cum = t if cum is None else cum + t
        dec_in = jnp.exp(cum)
        inv = jnp.exp(-cum)
        a_hat, b_hat, k_hat, r_hat = -kk * jnp.exp(cum - lw), kk * a * inv, kmod * inv, r * dec_in
        last = dec_in[:, chunk - 1:chunk, :]

        both = lambda x: jnp.concatenate([x, x], axis=0)
        split = lambda x: jnp.concatenate([jnp.where(lo_half, x, 0.0), jnp.where(hi_half, x, 0.0)], axis=0)
        merge = lambda x: jnp.where(lo_half, x[:g], x[g:])
        b2, k2, v2 = both(b_hat), both(k_hat), both(v)
        am = split(a_hat)
        n_mat = jnp.where(tril_strict, bdot(am, b2, bnt), 0.0)
        m_mat = jnp.where(tril_strict, bdot(am, k2, bnt), 0.0)
        inv_mat, pw = eye + n_mat, n_mat
        for _ in range(n_double):
            pw = bdot(pw, pw)
            inv_mat = inv_mat + bdot(inv_mat, pw)
        a2_scr[pi, pl.ds(c0, g)] = merge(bdot(inv_mat, both(a_hat)))
        u0_scr[pi, pl.ds(c0, g)] = merge(bdot(inv_mat, bdot(m_mat, v2)))
        rm = split(r_hat)
        no = jnp.where(tril_incl, bdot(rm, b2, bnt), 0.0)
        mo = jnp.where(tril_incl, bdot(rm, k2, bnt), 0.0)
        no_scr[pi, 0, pl.ds(c0, g)] = no[:g]
        no_scr[pi, 1, pl.ds(c0, g)] = no[g:]
        o0_scr[pi, pl.ds(c0, g)] = merge(bdot(mo, v2))
        rh_scr[pi, pl.ds(c0, g)] = r_hat
        b_last, k_last = b_hat * last, k_hat * last
        for j in range(g):
            bp_scr[pi, c0 + j] = b_last[j].T
            kv_scr[pi, c0 + j] = jnp.where(same_head, _dot3(k_last[j].T, v[j], _NN), 0.0)
            last_scr[pi, c0 + j] = jnp.broadcast_to(last[j], (blk, blk)).T

    def phase1(gi, _):
        for pi in range(n_pairs):
            prepare(gi, pi)
        return 0

    lax.fori_loop(0, n_chunks // grp_chunks, phase1, 0)

    for pi in range(n_pairs):
        s_scr[pi] = jnp.concatenate([jnp.concatenate([s0_ref[2 * pi], zero], axis=1),
                                     jnp.concatenate([zero, s0_ref[2 * pi + 1]], axis=1)], axis=0).T

    def advance(c, pi):
        sl = pl.ds(pl.multiple_of(c * chunk, chunk), chunk)
        state = s_scr[pi]
        u = u0_scr[pi, c] + _dot3(a2_scr[pi, c], state, _NN)
        s_scr[pi] = (jnp.where(same_head, state * last_scr[pi, c] + _dot3(bp_scr[pi, c], u, _NN), 0.0)
                     + kv_scr[pi, c])
        o = (_dot3(rh_scr[pi, c], state, _NN) + o0_scr[pi, c]
             + jnp.where(lo_half, _dot3(no_scr[pi, 0, c], u, _NN), _dot3(no_scr[pi, 1, c], u, _NN)))

        r, k, v, a, g = (ref[sl, lanes(pi)] for ref in (r_ref, k_ref, v_ref, a_ref, g_ref))
        _, k_a, r_k, ln_w, ln_b = params(pi)
        kmod = k * (1.0 + (a - 1.0) * k_a)
        mu = _half_sum(o, lo_half) / hs
        dev = o - mu
        var = _half_sum(dev * dev, lo_half) / hs
        o_n = dev * lax.rsqrt(var + LNX_EPS) * ln_w + ln_b
        bonus = _half_sum(r * kmod * r_k, lo_half) * v
        z_ref[sl, lanes(pi)] = (o_n + bonus) * g

    def phase2(c, _):
        for pi in range(n_pairs):
            advance(c, pi)
        return 0

    lax.fori_loop(0, n_chunks, phase2, 0)
    for pi in range(n_pairs):
        fin = s_scr[pi].T
        sT_ref[2 * pi] = fin[:hs, :hs]
        sT_ref[2 * pi + 1] = fin[hs:, hs:]


def _rwkv_scan(rkv, lag, params, state0, batch, seq, chunk, n_pairs):
    d = rkv.shape[-1] // 3
    blk = 2 * RWKV_HS
    width = n_pairs * blk
    groups = d // width
    n_chunks = seq // chunk

    def seq_spec(part):
        return pl.BlockSpec((None, seq, width), lambda b, p: (b, 0, part * groups + p))

    st_spec = pl.BlockSpec((None, 2 * n_pairs, RWKV_HS, RWKV_HS), lambda b, p: (b, p, 0, 0))
    body = functools.partial(_rwkv_scan_body, chunk=chunk, n_chunks=n_chunks, n_pairs=n_pairs)
    per_chunk = lambda *shape: pltpu.VMEM((n_pairs, n_chunks) + shape, F32)
    z, s_fin = pl.pallas_call(
        body,
        grid=(batch, groups),
        in_specs=[seq_spec(0), seq_spec(1), seq_spec(2), seq_spec(0), seq_spec(1), seq_spec(2),
                  pl.BlockSpec((SUBLANES, width), lambda b, p: (0, p)), st_spec],
        out_specs=[pl.BlockSpec((None, seq, width), lambda b, p: (b, 0, p)), st_spec],
        out_shape=[jax.ShapeDtypeStruct((batch, seq, d), F32),
                   jax.ShapeDtypeStruct(state0.shape, F32)],
        scratch_shapes=[pltpu.VMEM((n_pairs, blk, blk), F32),
                        per_chunk(chunk, blk), per_chunk(chunk, blk), per_chunk(chunk, blk),
                        per_chunk(chunk, blk), per_chunk(blk, chunk),
                        pltpu.VMEM((n_pairs, 2, n_chunks, chunk, chunk), F32),
                        per_chunk(blk, blk), per_chunk(blk, blk)],
        compiler_params=_cparams(("parallel", "parallel")),
        name="rwkv_scan",
    )(rkv, rkv, rkv, lag, lag, lag, params, state0)
    return z, s_fin


def _rwkv_step_body(rkv_ref, lag_ref, par_ref, s0_ref, z_ref, s1_ref):
    hs = RWKV_HS
    r, k, v = rkv_ref[0], rkv_ref[1], rkv_ref[2]
    lw, a, g = lag_ref[0], lag_ref[1], lag_ref[2]
    k_k, k_a, r_k, ln_w, ln_b = (par_ref[i] for i in range(5))
    eye = (lax.broadcasted_iota(jnp.int32, (hs, hs), 0) == lax.broadcasted_iota(jnp.int32, (hs, hs), 1))
    kk = k * k_k
    kk = kk / jnp.maximum(jnp.sqrt(jnp.sum(kk * kk, axis=-1, keepdims=True)), 1e-12)
    kmod = k * (1.0 + (a - 1.0) * k_a)
    state = s0_ref[...]
    sa = jnp.sum(state * (-kk), axis=-1, keepdims=True)
    v_col = jnp.sum(jnp.where(eye, v, 0.0), axis=-1, keepdims=True)
    new = state * jnp.exp(lw) + sa * (kk * a) + v_col * kmod
    s1_ref[...] = new
    o_col = jnp.sum(new * r, axis=-1, keepdims=True)
    o = jnp.sum(jnp.where(eye, o_col, 0.0), axis=1, keepdims=True)
    mu = jnp.mean(o, axis=-1, keepdims=True)
    dev = o - mu
    var = jnp.mean(dev * dev, axis=-1, keepdims=True)
    o_n = dev * lax.rsqrt(var + LNX_EPS) * ln_w + ln_b
    bonus = jnp.sum(r * kmod * r_k, axis=-1, keepdims=True) * v
    z_ref[...] = (o_n + bonus) * g


def _rwkv_step(rkv, lag, params, state0):
    b, heads, hs, _ = state0.shape
    vec = lambda x: x.reshape(b, 3, heads, 1, hs)
    vec_spec = pl.BlockSpec((None, 3, heads, 1, hs), lambda i: (i, 0, 0, 0, 0))
    st_spec = pl.BlockSpec((None, heads, hs, hs), lambda i: (i, 0, 0, 0))
    z, s_fin = pl.pallas_call(
        _rwkv_step_body,
        grid=(b,),
        in_specs=[vec_spec, vec_spec, pl.BlockSpec((SUBLANES, heads, 1, hs), lambda i: (0, 0, 0, 0)), st_spec],
        out_specs=[pl.BlockSpec((None, heads, 1, hs), lambda i: (i, 0, 0, 0)), st_spec],
        out_shape=[jax.ShapeDtypeStruct((b, heads, 1, hs), F32), jax.ShapeDtypeStruct(state0.shape, F32)],
        compiler_params=_cparams(("parallel",)),
        name="rwkv_step",
    )(vec(rkv), vec(lag), params.reshape(SUBLANES, heads, 1, hs), state0)
    return z.reshape(b, heads * hs), s_fin


def _peer_route_body(q_ref, k1_ref, k2_ref, ids_ref, gate_ref):
    tm = q_ref.shape[0]
    topk = PEER_TOPK
    n_cand = topk * topk
    half = k1_ref.shape[-1]
    key_row = lax.broadcasted_iota(jnp.int32, (N_KEYS, tm), 0).astype(F32)
    assert topk == 2 * SUBLANES
    n_kept = topk + (SUBLANES - 1) * SUBLANES + SUBLANES
    kept = lax.broadcasted_iota(jnp.int32, (n_kept, tm), 0).astype(F32)
    tail = kept - (n_kept - SUBLANES)
    cand_code = jnp.where(tail < 0, kept + SUBLANES * jnp.floor((kept - topk) * (1.0 / SUBLANES)),
                          (tail + SUBLANES) * topk)
    cand_code = jnp.where(kept < topk + SUBLANES, kept, cand_code)
    rank = lax.broadcasted_iota(jnp.int32, (topk, tm), 0)
    rank_f = rank.astype(F32)

    def take_max(s, rows, n_rows):
        m = jnp.max(s, axis=0, keepdims=True)
        idx = jnp.min(jnp.where(s == m, rows, float(n_rows)), axis=0, keepdims=True)
        return m, idx, jnp.where(rows == idx, -jnp.inf, s)

    vals_heads, ids_heads = [], []
    for hq in range(PEER_HEADS):
        q1 = q_ref[:, hq * 2 * half:hq * 2 * half + half].astype(BF16)
        q2 = q_ref[:, hq * 2 * half + half:(hq + 1) * 2 * half].astype(BF16)
        s1 = lax.dot_general(k1_ref[hq].astype(BF16), q1, _NT, preferred_element_type=F32)
        s2 = lax.dot_general(k2_ref[hq].astype(BF16), q2, _NT, preferred_element_type=F32)

        def sub_it(i, carry):
            s1, s2, v1, e1, v2, e2 = carry
            m1, i1, s1 = take_max(s1, key_row, N_KEYS)
            m2, i2, s2 = take_max(s2, key_row, N_KEYS)
            here = rank == i
            return (s1, s2, jnp.where(here, m1, v1), jnp.where(here, i1, e1),
                    jnp.where(here, m2, v2), jnp.where(here, i2, e2))

        zero = jnp.zeros((topk, tm), F32)
        _, _, v1, e1, v2, e2 = lax.fori_loop(0, topk, sub_it, (s1, s2, zero, zero, zero, zero))
        cs = jnp.concatenate([v1[0:1, :] + v2] + [v1[i:i + 1, :] + v2[:SUBLANES] for i in range(1, SUBLANES)]
                             + [v1[SUBLANES:, :] + v2[0:1, :]], axis=0)

        def cand_it(i, carry):
            cs, vals, ids = carry
            m, pos, cs = take_max(cs, cand_code, n_cand)
            hi = jnp.floor(pos * (1.0 / topk))
            lo = pos - hi * topk
            k_hi = jnp.sum(jnp.where(rank_f == hi, e1, 0.0), axis=0, keepdims=True)
            k_lo = jnp.sum(jnp.where(rank_f == lo, e2, 0.0), axis=0, keepdims=True)
            here = rank == i
            return cs, jnp.where(here, m, vals), jnp.where(here, k_hi * N_KEYS + k_lo, ids)

        _, vals, ids = lax.fori_loop(0, topk, cand_it, (cs, zero, zero))
        ex = jnp.exp(vals - jnp.max(vals, axis=0, keepdims=True))
        vals_heads.append(ex / jnp.sum(ex, axis=0, keepdims=True))
        ids_heads.append(ids)

    gate_ref[...] = jnp.concatenate(vals_heads, axis=0).T
    ids_ref[...] = jnp.concatenate(ids_heads, axis=0).T.astype(jnp.int32)


def _peer_route(q, k1, k2, tm):
    m, dq = q.shape
    assert PEER_HEADS * PEER_TOPK == N_KEYS
    kspec = pl.BlockSpec(k1.shape, lambda i: (0, 0, 0))
    ospec = pl.BlockSpec((tm, N_KEYS), lambda i: (i, 0))
    return pl.pallas_call(
        _peer_route_body,
        grid=(m // tm,),
        in_specs=[pl.BlockSpec((tm, dq), lambda i: (i, 0)), kspec, kspec],
        out_specs=[ospec, ospec],
        out_shape=[jax.ShapeDtypeStruct((m, N_KEYS), jnp.int32), jax.ShapeDtypeStruct((m, N_KEYS), F32)],
        compiler_params=_cparams(("parallel",)),
        name="peer_route",
    )(q, k1, k2)


def _gelu(x):
    return 0.5 * x * (1.0 + lax.erf(x * (2.0 ** -0.5)))


PEER_BUFFERS = 4


def _peer_expert_body(ids_ref, nxt_ref, h_ref, gate_ref, x_ref, g5_ref, uv_hbm, o_ref, *scratch, tt):
    nb = PEER_BUFFERS
    bufs, (sem, pout) = scratch[:nb], scratch[nb:]
    d = h_ref.shape[1]
    n_sel = gate_ref.shape[1]
    step = pl.program_id(0)
    last_step = pl.num_programs(0) - 1
    diag = (lax.broadcasted_iota(jnp.int32, (n_sel, n_sel), 0)
            == lax.broadcasted_iota(jnp.int32, (n_sel, n_sel), 1))

    n_rows = uv_hbm.shape[1] // 2
    seg = uv_hbm.shape[2]

    def issue(src_ref, row, slot):
        for p in range(n_sel):
            pltpu.make_async_copy(uv_hbm.at[src_ref[row, p]], bufs[slot].at[:, p], sem.at[slot]).start()

    def wait_all(slot):
        pltpu.make_async_copy(bufs[slot], bufs[slot], sem.at[slot]).wait()

    def compute(t, slot):
        part = None
        hrow = h_ref[pl.ds(t, 1), :]
        for r in range(n_rows):
            term = bufs[slot][r] * hrow[:, r * seg:(r + 1) * seg]
            part = term if part is None else part + term
        act = jnp.sum(part, axis=1, keepdims=True)
        gate = jnp.sum(jnp.where(diag, gate_ref[pl.ds(t, 1), :], 0.0), axis=1, keepdims=True)
        wgt = _gelu(act) * gate
        pout[pl.ds(t, 1), :] = jnp.concatenate(
            [jnp.sum(bufs[slot][n_rows + r] * wgt, axis=0, keepdims=True) for r in range(n_rows)], axis=1)

    ahead = nb - 1

    @pl.when(step == 0)
    def _():
        for j in range(ahead):
            issue(ids_ref, j, j)

    def token_group(k, _):
        for j in range(nb):
            t = nb * k + j
            issue(ids_ref, t + ahead, (j + ahead) % nb)
            wait_all(j)
            compute(t, j)
        return 0

    lax.fori_loop(0, tt // nb - 1, token_group, 0)
    for j in range(nb):
        t = tt - nb + j
        if t + ahead < tt:
            issue(ids_ref, t + ahead, (j + ahead) % nb)
        else:
            issue(nxt_ref, t + ahead - tt, (j + ahead) % nb)
        wait_all(j)
        compute(t, j)

    @pl.when(step == last_step)
    def _():
        for j in range(ahead):
            wait_all(j)

    o_ref[...] = x_ref[...] + g5_ref[...] * pout[...]


def _peer_experts(grp, h, ids, gates, x, gate5, uv, tt):
    m, d = h.shape
    n_sel = ids.shape[1]
    tt = min(tt, grp.tm)
    assert grp.tm % tt == 0 and tt % PEER_BUFFERS == 0
    sub = grp.tm // tt
    n_steps = m // tt
    ids_spec = lambda nxt: pl.BlockSpec((tt, n_sel), lambda i: (jnp.minimum(i + nxt, n_steps - 1), 0),
                                        memory_space=pltpu.SMEM)
    tile = lambda width: pl.BlockSpec((tt, width), lambda i: (i, 0))
    if grp.seq == 1:
        g5, g5_spec = gate5, tile(d)
    else:
        tps = grp.tiles_per_seq * sub
        g5, g5_spec = gate5[:, None, :], pl.BlockSpec((None, 1, d), lambda i: (i // tps, 0, 0))
    return pl.pallas_call(
        functools.partial(_peer_expert_body, tt=tt),
        grid=(n_steps,),
        in_specs=[ids_spec(0), ids_spec(1),
                  tile(d), tile(n_sel), tile(d), g5_spec,
                  pl.BlockSpec(memory_space=pl.ANY)],
        out_specs=tile(d),
        out_shape=jax.ShapeDtypeStruct((m, d), F32),
        scratch_shapes=[pltpu.VMEM((uv.shape[1], n_sel, uv.shape[2]), F32)] * PEER_BUFFERS
        + [pltpu.SemaphoreType.DMA((PEER_BUFFERS,)), pltpu.VMEM((tt, d), F32)],
        compiler_params=_cparams(("arbitrary",)),
        name="peer_experts",
    )(ids, ids, h, gates, x, g5, uv)


def _peer(grp, x, mod, norm_g, w_q, k1, k2, uv):
    h = _normmod(grp, x, norm_g, mod[:, 3], mod[:, 4])
    m, d = h.shape
    tn = min(1024, w_q.shape[1])
    (q,) = _mm(m, grp.tm, w_q, tn, [h], [_tile_spec(grp.tm, d)], [], [],
               [jax.ShapeDtypeStruct((m, w_q.shape[1]), F32)], [_out_spec(grp.tm, tn)],
               _ident, _epi(lambda acc, j: acc), "peer_q")
    ids, gates = _peer_route(q, k1, k2, min(128, m))
    return _peer_experts(grp, h, ids, gates, x, mod[:, 5], uv, 32)


def _pad_cols(w, width):
    return jnp.pad(w, ((0, 0), (0, width - w.shape[1])))


def _pad_rows(w, height):
    return jnp.pad(w, ((0, height - w.shape[0]), (0, 0)))


def kernel(x_prompt, x_sample, cache_k, cache_v, cache_logf, state_wkv, state_shift, page_table, c_prompt, c_sample, ada_w, ada_b, norm_mix, norm_ffn, att_w_in, att_w_o, att_q_gain, att_k_gain, att_f_bias, rw_mix, rw_w0, rw_w1, rw_w2, rw_a0, rw_a1, rw_a2, rw_g1, rw_g2, rw_k_k, rw_k_a, rw_r_k, rw_w_r, rw_w_k, rw_w_v, rw_w_o, rw_ln_w, rw_ln_b, peer_wq, peer_k1, peer_k2, peer_u, peer_v):
    bp, tp, d = x_prompt.shape
    bs = x_sample.shape[0]
    depth = ada_w.shape[0]
    d_att = H_ATT * HEAD_DIM
    groups = (_Group(bp, tp, 512), _Group(bs, 1, 128))
    xs = [x_prompt.reshape(bp * tp, d), x_sample.reshape(bs, d)]

    c_all = jnp.concatenate([c_prompt, c_sample], axis=0)
    n_seq = c_all.shape[0]
    c_all = jnp.pad(c_all, ((0, (-n_seq) % SUBLANES), (0, 0)))

    outs = {name: [[], []] for name in ("k", "v", "lf", "wkv", "shift")}
    for l in range(depth):
        mod_all = _ada_mod(c_all, ada_w[l], ada_b[l])[:n_seq].reshape(n_seq, N_MOD, d)
        mods = (mod_all[:bp], mod_all[bp:])
        i = l // 2
        if l % 2 == 0:
            w_qkv = att_w_in[i][:, :3 * d_att].astype(BF16)
            w_f = _pad_cols(att_w_in[i][:, 3 * d_att:], LANES).astype(BF16)
            f_bias = _pad_cols(att_f_bias[i][None, :], LANES)
            w_o = att_w_o[i].astype(BF16)
            for gi, grp in enumerate(groups):
                x, mod = xs[gi], mods[gi]
                h = _normmod(grp, x, norm_mix[l], mod[:, 0], mod[:, 1])
                q, k, v, logf = _att_project(grp, h, w_qkv, w_f, att_q_gain[i], att_k_gain[i], f_bias)
                if grp.seq > 1:
                    fcum = _cumsum_seq(logf.reshape(grp.batch, grp.seq, LANES))
                    fcum_t = jnp.swapaxes(fcum[:, :, :H_FOX], 1, 2)
                    o = _attn_prompt(q, k, v, fcum, fcum_t, grp.batch, grp.seq)
                else:
                    o = _attn_decode(q, k, v, logf, cache_k[i], cache_v[i], cache_logf[i], page_table)
                xs[gi] = _proj_residual(grp, o, w_o, x, mod[:, 2], "att_out")
                outs["k"][gi].append(k.reshape(grp.batch, grp.seq, H_ATT, HEAD_DIM))
                outs["v"][gi].append(v.reshape(grp.batch, grp.seq, H_ATT, HEAD_DIM))
                outs["lf"][gi].append(logf[:, :H_FOX].reshape(grp.batch, grp.seq, H_FOX))
        else:
            r_decay, r_aaa = rw_w1.shape[2], rw_a1.shape[2]
            w_rkv = jnp.concatenate([rw_w_r[i], rw_w_k[i], rw_w_v[i]], axis=1).astype(BF16)
            w_lr1 = jnp.concatenate([_pad_cols(rw_w1[i], LANES), _pad_cols(rw_a1[i], LANES), rw_g1[i]],
                                    axis=1).astype(BF16)
            n1 = w_lr1.shape[1]
            w_lr2 = jnp.zeros((n1, 3 * d), F32)
            w_lr2 = w_lr2.at[:r_decay, :d].set(rw_w2[i])
            w_lr2 = w_lr2.at[LANES:LANES + r_aaa, d:2 * d].set(rw_a2[i])
            w_lr2 = w_lr2.at[2 * LANES:, 2 * d:].set(rw_g2[i]).astype(BF16)
            bias_lr2 = jnp.concatenate([rw_w0[i], rw_a0[i], jnp.zeros((d,), F32)])[None, :]
            params = _pad_rows(jnp.stack([rw_k_k[i], rw_k_a[i], rw_r_k[i].reshape(d), rw_ln_w[i], rw_ln_b[i]]),
                               SUBLANES)
            w_o = rw_w_o[i].astype(BF16)
            for gi, grp in enumerate(groups):
                x, mod = xs[gi], mods[gi]
                h = _normmod(grp, x, norm_mix[l], mod[:, 0], mod[:, 1])
                h3 = h.reshape(grp.batch, grp.seq, d)
                if grp.seq > 1:
                    prev = jnp.concatenate([jnp.zeros((grp.batch, 1, d), F32), h3[:, :-1]], axis=1)
                else:
                    prev = state_shift[i][:, None, :]
                rkv, lag = _rwkv_project(grp, h, prev.reshape(grp.m, d), rw_mix[i], w_rkv, w_lr1, w_lr2, bias_lr2)
                if grp.seq > 1:
                    state0 = jnp.zeros((grp.batch, d // RWKV_HS, RWKV_HS, RWKV_HS), F32)
                    seq3 = lambda a: a.reshape(grp.batch, grp.seq, 3 * d)
                    z, s_fin = _rwkv_scan(seq3(rkv), seq3(lag), params, state0, grp.batch, grp.seq,
                                          min(64, grp.seq), 1)
                    z = z.reshape(grp.m, d)
                else:
                    z, s_fin = _rwkv_step(rkv, lag, params, state_wkv[i])
                xs[gi] = _proj_residual(grp, z, w_o, x, mod[:, 2], "rwkv_out")
                outs["wkv"][gi].append(s_fin)
                outs["shift"][gi].append(h3[:, -1])
        n_exp = peer_u.shape[1]
        slab = lambda w: w.reshape(n_exp, d // LANES, LANES)
        uv = jnp.concatenate([slab(peer_u[l]), slab(peer_v[l])], axis=1)
        w_q = peer_wq[l].astype(BF16)
        for gi, grp in enumerate(groups):
            xs[gi] = _peer(grp, xs[gi], mods[gi], norm_ffn[l], w_q, peer_k1[l], peer_k2[l], uv)

    st = lambda name, gi: jnp.stack(outs[name][gi])
    return (xs[0].reshape(bp, tp, d), xs[1].reshape(bs, 1, d),
            st("k", 0), st("v", 0), st("lf", 0), st("wkv", 0), st("shift", 0),
            st("k", 1), st("v", 1), st("lf", 1), st("wkv", 1), st("shift", 1))
```
